```python
import numpy as np
import jax, jax.numpy as jnp
from jax import lax

D_MODEL = 1024
BATCH = 4
SEQ = 4096
DEPTH = 1
DEC_BATCH = 128
DEC_SEQ = 8
PAST_LEN = 8192
PAGE_SIZE = 128

C_CONV = 512
CONV_K = 31
N_HEADS = 8
N_KV = 2
HEAD_DIM = 64
HPG = N_HEADS // N_KV
CMP_BLOCK = 32
CMP_STRIDE = 16
CMP_RATIO = CMP_BLOCK // CMP_STRIDE
SEL_BLOCK = 64
N_SEL = 16
WINDOW = 512
Q_BLOCK = 128
ROPE_THETA = 10000.0
FORCE_BONUS = 1000.0
NEG_INF = -1e30
N_GROUPS = 4
EXP_PER_GROUP = 8
N_EXPERTS = N_GROUPS * EXP_PER_GROUP
TOP_K_IN_GROUP = 2
D_EXPERT = 128
EPS = 1e-6
KV_W = N_KV * HEAD_DIM
IN_SPLITS = (2 * C_CONV, N_HEADS * HEAD_DIM, KV_W, KV_W, KV_W, KV_W, KV_W, KV_W, 3 * N_HEADS, 2 * D_MODEL)
IN_COLS = sum(IN_SPLITS)

kernel_name = 'hybrid_conformer_nsa_hmoe_step'


def rms_norm(x, g):
    xf = x.astype(jnp.float32)
    y = xf * lax.rsqrt(jnp.mean(xf * xf, axis=-1, keepdims=True) + EPS)
    return (y * g.astype(jnp.float32)).astype(x.dtype)


def rope(x, pos):
    half = HEAD_DIM // 2
    inv_freq = ROPE_THETA ** (-jnp.arange(half, dtype=jnp.float32) * 2.0 / HEAD_DIM)
    ang = pos.astype(jnp.float32)[:, None] * inv_freq[None, :]
    cos = jnp.cos(ang)[None, :, None, :]
    sin = jnp.sin(ang)[None, :, None, :]
    xf = x.astype(jnp.float32)
    x1, x2 = xf[..., :half], xf[..., half:]
    return jnp.concatenate([x1 * cos - x2 * sin, x2 * cos + x1 * sin], axis=-1).astype(x.dtype)


def project_in(xn, w_in):
    B, T, _ = xn.shape
    cuts = [int(c) for c in np.cumsum(IN_SPLITS)[:-1]]
    u, q, kc, vc, ks, vs, kw, vw, ng, mg = jnp.split(xn @ w_in, cuts, axis=-1)
    heads = lambda a, n: a.reshape(B, T, n, HEAD_DIM)
    return (u, heads(q, N_HEADS), heads(kc, N_KV), heads(vc, N_KV), heads(ks, N_KV), heads(vs, N_KV),
            heads(kw, N_KV), heads(vw, N_KV),
            jax.nn.sigmoid(ng.reshape(B, T, N_HEADS, 3)),
            jax.nn.sigmoid(mg.reshape(B, T, 2, D_MODEL)))


def conv_module(u, buf, w_dw, b_dw, n_g, n_b, w_out):
    a = u[..., :C_CONV] * jax.nn.sigmoid(u[..., C_CONV:])
    full = jnp.concatenate([buf, a], axis=1)
    c = lax.conv_general_dilated(full, w_dw[:, None, :], (1,), 'VALID',
                                 dimension_numbers=('NWC', 'WIO', 'NWC'),
                                 feature_group_count=C_CONV) + b_dw
    cf = c.astype(jnp.float32)
    mu = jnp.mean(cf, axis=-1, keepdims=True)
    var = jnp.mean(jnp.square(cf - mu), axis=-1, keepdims=True)
    cn = (cf - mu) * lax.rsqrt(var + EPS) * n_g.astype(jnp.float32) + n_b.astype(jnp.float32)
    y = jax.nn.silu(cn).astype(u.dtype) @ w_out
    return y, full[:, -(CONV_K - 1):]


def compress(k_raw, w1, b1, w2):
    B, T, G, Dh = k_raw.shape
    nb = (T - CMP_BLOCK) // CMP_STRIDE + 1
    n_str = nb + CMP_RATIO - 1
    s = k_raw[:, :n_str * CMP_STRIDE].reshape(B, n_str, CMP_STRIDE, G, Dh)
    w1r = w1.reshape(CMP_RATIO, CMP_STRIDE, Dh, Dh)
    pre = jnp.einsum('bnlgd,lde->bnge', s[:, 0:nb], w1r[0]) + b1
    for r in range(1, CMP_RATIO):
        pre = pre + jnp.einsum('bnlgd,lde->bnge', s[:, r:r + nb], w1r[r])
    return jnp.einsum('bnge,ef->bngf', jax.nn.silu(pre), w2)


def block_overlap(nc, ns):
    i = np.arange(nc)[:, None]
    j = np.arange(ns)[None, :]
    ov = (i * CMP_STRIDE < (j + 1) * SEL_BLOCK) & (i * CMP_STRIDE + CMP_BLOCK > j * SEL_BLOCK)
    return jnp.asarray(ov.astype(np.float32))


def nsa_attend(q, q_rot, pos, kc, vc, gather_sel, ns, kw, vw, pos_w, gates):
    B, Q = q.shape[0], q.shape[1]
    scale = HEAD_DIM ** -0.5
    qg = q.reshape(B, Q, N_KV, HPG, HEAD_DIM)
    qrg = q_rot.reshape(B, Q, N_KV, HPG, HEAD_DIM)
    t = pos
    nc = kc.shape[1]
    c_end = jnp.arange(nc) * CMP_STRIDE + CMP_BLOCK - 1
    c_ok = c_end[None, :] <= t[:, None]
    s_c = jnp.einsum('bqghd,bngd->bqghn', qg, kc).astype(jnp.float32) * scale
    s_c = jnp.where(c_ok[None, :, None, None, :], s_c, NEG_INF)
    p_c = jax.nn.softmax(s_c, axis=-1) * jnp.any(c_ok, axis=-1)[None, :, None, None, None]
    o_c = jnp.einsum('bqghn,bngd->bqghd', p_c.astype(vc.dtype), vc)
    imp = jnp.einsum('bqghn,nj->bqgj', p_c, block_overlap(nc, ns))
    j = jnp.arange(ns)
    cur = t // SEL_BLOCK
    forced = (j[None, :] == 0) | (j[None, :] == cur[:, None]) | (j[None, :] == cur[:, None] - 1)
    blk_ok = j[None, :] * SEL_BLOCK <= t[:, None]
    imp = jnp.where(blk_ok[None, :, None, :], imp + FORCE_BONUS * forced[None, :, None, :], NEG_INF)
    _, idx = lax.top_k(imp, min(N_SEL, ns))
    ks, vs = gather_sel(idx)
    tok = idx[..., None] * SEL_BLOCK + jnp.arange(SEL_BLOCK)
    tok_ok = tok <= t[None, :, None, None, None]
    s_s = jnp.einsum('bqghd,bqgkld->bqghkl', qrg, ks).astype(jnp.float32) * scale
    s_s = jnp.where(tok_ok[:, :, :, None], s_s, NEG_INF)
    p_s = jax.nn.softmax(s_s.reshape(B, Q, N_KV, HPG, -1), axis=-1).reshape(s_s.shape)
    o_s = jnp.einsum('bqghkl,bqgkld->bqghd', p_s.astype(vs.dtype), vs)
    w_ok = (pos_w[None, :] <= t[:, None]) & (t[:, None] - pos_w[None, :] < WINDOW) & (pos_w[None, :] >= 0)
    s_w = jnp.einsum('bqghd,bwgd->bqghw', qrg, kw).astype(jnp.float32) * scale
    s_w = jnp.where(w_ok[None, :, None, None, :], s_w, NEG_INF)
    p_w = jax.nn.softmax(s_w, axis=-1)
    o_w = jnp.einsum('bqghw,bwgd->bqghd', p_w.astype(vw.dtype), vw)
    g = gates.reshape(B, Q, N_KV, HPG, 3)
    o = g[..., 0:1] * o_c + g[..., 1:2] * o_s + g[..., 2:3] * o_w
    return o.reshape(B, Q, N_HEADS, HEAD_DIM)


def merge_out(y_conv, y_nsa, mg, w_o):
    return (mg[:, :, 0] * y_conv + mg[:, :, 1] * y_nsa) @ w_o


def mixer_prompt(xn, w_in, conv_w, cmp_w, w_nsa_out, w_o):
    B, T, _ = xn.shape
    pos = jnp.arange(T)
    u, q, kc_r, vc_r, ks_r, vs, kw_r, vw, ng, mg = project_in(xn, w_in)
    y_conv, conv_state = conv_module(u, jnp.zeros((B, CONV_K - 1, C_CONV), u.dtype), *conv_w)
    q_rot, ks, kw = rope(q, pos), rope(ks_r, pos), rope(kw_r, pos)
    kc = compress(kc_r, *cmp_w[:3])
    vc = compress(vc_r, *cmp_w[3:])
    ns = T // SEL_BLOCK
    kb = ks.reshape(B, ns, SEL_BLOCK, N_KV, HEAD_DIM)
    vb = vs.reshape(B, ns, SEL_BLOCK, N_KV, HEAD_DIM)
    b_ix = jnp.arange(B)[:, None, None, None]
    g_ix = jnp.arange(N_KV)[None, None, :, None]

    def gather_sel(idx):
        return kb[b_ix, idx, :, g_ix, :], vb[b_ix, idx, :, g_ix, :]

    pad = ((0, 0), (WINDOW, 0), (0, 0), (0, 0))
    kw_pad, vw_pad = jnp.pad(kw, pad), jnp.pad(vw, pad)

    def block(i):
        start = i * Q_BLOCK
        sl = lambda a: lax.dynamic_slice_in_dim(a, start, Q_BLOCK, axis=1)
        kwb = lax.dynamic_slice_in_dim(kw_pad, start, Q_BLOCK + WINDOW, axis=1)
        vwb = lax.dynamic_slice_in_dim(vw_pad, start, Q_BLOCK + WINDOW, axis=1)
        pos_b = start + jnp.arange(Q_BLOCK)
        pos_w = start - WINDOW + jnp.arange(Q_BLOCK + WINDOW)
        return nsa_attend(sl(q), sl(q_rot), pos_b, kc, vc, gather_sel, ns, kwb, vwb, pos_w, sl(ng))

    o = lax.map(block, jnp.arange(T // Q_BLOCK))
    o = jnp.moveaxis(o, 0, 1).reshape(B, T, N_HEADS * HEAD_DIM)
    out = merge_out(y_conv, o @ w_nsa_out, mg, w_o)
    wb = min(WINDOW, T)
    return out, (kc_r, vc_r, ks, vs, kw[:, T - wb:], vw[:, T - wb:], conv_state)


def mixer_sample(xn, ck_cmp, cv_cmp, ck_sel, cv_sel, sk_win, sv_win, s_conv, page_table,
                 w_in, conv_w, cmp_w, w_nsa_out, w_o):
    B, T, _ = xn.shape
    past = page_table.shape[1] * PAGE_SIZE
    pos = past + jnp.arange(T)
    u, q, kc_r, vc_r, ks_r, vs, kw_r, vw, ng, mg = project_in(xn, w_in)
    y_conv, conv_state = conv_module(u, s_conv, *conv_w)
    q_rot, ks, kw = rope(q, pos), rope(ks_r, pos), rope(kw_r, pos)

    def paged(pool):
        return pool[page_table].reshape(B, past, N_KV, HEAD_DIM)

    kc = compress(jnp.concatenate([paged(ck_cmp), kc_r], axis=1), *cmp_w[:3])
    vc = compress(jnp.concatenate([paged(cv_cmp), vc_r], axis=1), *cmp_w[3:])
    ns_past = past // SEL_BLOCK
    n_new = -(-T // SEL_BLOCK)
    bpp = PAGE_SIZE // SEL_BLOCK
    pool_blocks = lambda p: p.reshape(p.shape[0], bpp, SEL_BLOCK, N_KV, HEAD_DIM)
    pad_new = ((0, 0), (0, n_new * SEL_BLOCK - T), (0, 0), (0, 0))
    new_blocks = lambda a: jnp.pad(a, pad_new).reshape(B, n_new, SEL_BLOCK, N_KV, HEAD_DIM)
    kp, vp = pool_blocks(ck_sel), pool_blocks(cv_sel)
    kn, vn = new_blocks(ks), new_blocks(vs)
    b_ix = jnp.arange(B)[:, None, None, None]
    g_ix = jnp.arange(N_KV)[None, None, :, None]

    def gather_sel(idx):
        in_past = (idx < ns_past)[..., None, None]
        jp = jnp.minimum(idx, ns_past - 1)
        jn = jnp.clip(idx - ns_past, 0, n_new - 1)
        page = page_table[b_ix, jp // bpp]
        sub = jp % bpp
        k_g = jnp.where(in_past, kp[page, sub, :, g_ix, :], kn[b_ix, jn, :, g_ix, :])
        v_g = jnp.where(in_past, vp[page, sub, :, g_ix, :], vn[b_ix, jn, :, g_ix, :])
        return k_g, v_g

    wb = sk_win.shape[1]
    kw_all = jnp.concatenate([sk_win, kw], axis=1)
    vw_all = jnp.concatenate([sv_win, vw], axis=1)
    pos_w = past - wb + jnp.arange(wb + T)
    o = nsa_attend(q, q_rot, pos, kc, vc, gather_sel, ns_past + n_new, kw_all, vw_all, pos_w, ng)
    out = merge_out(y_conv, o.reshape(B, T, N_HEADS * HEAD_DIM) @ w_nsa_out, mg, w_o)
    return out, (kc_r, vc_r, ks, vs, kw_all[:, T:], vw_all[:, T:], conv_state)


def hier_moe(xn, w_rg, b_rg, w_re, b_re, w_gate, w_up, w_down):
    shp = xn.shape
    xf = xn.reshape(-1, D_MODEL)
    lg = (xf @ w_rg).astype(jnp.float32) + b_rg.astype(jnp.float32)
    p_grp = jax.nn.softmax(lg, axis=-1)
    g_star = jnp.argmax(lg, axis=-1)
    le = ((xf @ w_re).astype(jnp.float32) + b_re.astype(jnp.float32)).reshape(-1, N_GROUPS, EXP_PER_GROUP)
    le_g = jnp.take_along_axis(le, g_star[:, None, None], axis=1)[:, 0]
    top_v, top_i = lax.top_k(le_g, TOP_K_IN_GROUP)
    w_tok = jax.nn.softmax(top_v, axis=-1) * jnp.take_along_axis(p_grp, g_star[:, None], axis=1)
    eid = g_star[:, None] * EXP_PER_GROUP + top_i
    comb = jnp.sum(jax.nn.one_hot(eid, N_EXPERTS, dtype=jnp.float32) * w_tok[..., None], axis=1)
    hid = jax.nn.silu(jnp.einsum('nd,edf->nef', xf, w_gate)) * jnp.einsum('nd,edf->nef', xf, w_up)
    y = jnp.einsum('nef,efd->nd', hid * comb[..., None].astype(hid.dtype), w_down)
    return y.reshape(shp)


def setup_inputs(seed: int = 0) -> dict:
    key = jax.random.key(seed)
    keys = jax.random.split(key, 40)
    n_pages = PAST_LEN // PAGE_SIZE
    n_phys = (DEC_BATCH * n_pages * 5) // 4
    wb = min(WINDOW, PAST_LEN)
    L = DEPTH

    def nrm(i, shape, scale):
        return jax.random.normal(keys[i], shape, jnp.float32) * scale

    pool = (L, n_phys, PAGE_SIZE, N_KV, HEAD_DIM)
    page_table = jax.random.permutation(keys[9], n_phys)[:DEC_BATCH * n_pages].reshape(DEC_BATCH, n_pages).astype(jnp.int32)
    return {
        'x_prompt': nrm(0, (BATCH, SEQ, D_MODEL), 1.0),
        'x_sample': nrm(1, (DEC_BATCH, DEC_SEQ, D_MODEL), 1.0),
        'cache_k_cmp': nrm(2, pool, 1.0),
        'cache_v_cmp': nrm(3, pool, 1.0),
        'cache_k_sel': nrm(4, pool, 1.0),
        'cache_v_sel': nrm(5, pool, 1.0),
        'state_k_win': nrm(6, (L, DEC_BATCH, wb, N_KV, HEAD_DIM), 1.0),
        'state_v_win': nrm(7, (L, DEC_BATCH, wb, N_KV, HEAD_DIM), 1.0),
        'state_conv': nrm(8, (L, DEC_BATCH, CONV_K - 1, C_CONV), 0.5),
        'page_table': page_table,
        'g_mix': 1.0 + nrm(10, (L, D_MODEL), 0.01),
        'w_in': nrm(11, (L, D_MODEL, IN_COLS), D_MODEL ** -0.5),
        'w_dw': nrm(12, (L, CONV_K, C_CONV), CONV_K ** -0.5),
        'b_dw': nrm(13, (L, C_CONV), 0.01),
        'conv_norm_g': 1.0 + nrm(14, (L, C_CONV), 0.01),
        'conv_norm_b': nrm(15, (L, C_CONV), 0.01),
        'w_conv_out': nrm(16, (L, C_CONV, D_MODEL), C_CONV ** -0.5),
        'w_ck1': nrm(17, (L, CMP_BLOCK, HEAD_DIM, HEAD_DIM), (CMP_BLOCK * HEAD_DIM) ** -0.5),
        'b_ck1': nrm(18, (L, HEAD_DIM), 0.01),
        'w_ck2': nrm(19, (L, HEAD_DIM, HEAD_DIM), HEAD_DIM ** -0.5),
        'w_cv1': nrm(20, (L, CMP_BLOCK, HEAD_DIM, HEAD_DIM), (CMP_BLOCK * HEAD_DIM) ** -0.5),
        'b_cv1': nrm(21, (L, HEAD_DIM), 0.01),
        'w_cv2': nrm(22, (L, HEAD_DIM, HEAD_DIM), HEAD_DIM ** -0.5),
        'w_nsa_out': nrm(23, (L, N_HEADS * HEAD_DIM, D_MODEL), (N_HEADS * HEAD_DIM) ** -0.5),
        'w_o': nrm(24, (L, D_MODEL, D_MODEL), D_MODEL ** -0.5),
        'g_ffn': 1.0 + nrm(25, (L, D_MODEL), 0.01),
        'w_rg': nrm(26, (L, D_MODEL, N_GROUPS), D_MODEL ** -0.5),
        'b_rg': nrm(27, (L, N_GROUPS), 0.01),
        'w_re': nrm(28, (L, D_MODEL, N_EXPERTS), D_MODEL ** -0.5),
        'b_re': nrm(29, (L, N_EXPERTS), 0.01),
        'w_gate': nrm(30, (L, N_EXPERTS, D_MODEL, D_EXPERT), D_MODEL ** -0.5),
        'w_up': nrm(31, (L, N_EXPERTS, D_MODEL, D_EXPERT), D_MODEL ** -0.5),
        'w_down': nrm(32, (L, N_EXPERTS, D_EXPERT, D_MODEL), D_EXPERT ** -0.5),
        'g_final': 1.0 + nrm(33, (D_MODEL,), 0.01),
    }


def reference(x_prompt, x_sample, cache_k_cmp, cache_v_cmp, cache_k_sel, cache_v_sel, state_k_win, state_v_win,
              state_conv, page_table, g_mix, w_in, w_dw, b_dw, conv_norm_g, conv_norm_b, w_conv_out,
              w_ck1, b_ck1, w_ck2, w_cv1, b_cv1, w_cv2, w_nsa_out, w_o, g_ffn, w_rg, b_rg, w_re, b_re,
              w_gate, w_up, w_down, g_final):
    h_p, h_s = x_prompt, x_sample
    per_layer = []
    for l in range(DEPTH):
        conv_w = (w_dw[l], b_dw[l], conv_norm_g[l], conv_norm_b[l], w_conv_out[l])
        cmp_w = (w_ck1[l], b_ck1[l], w_ck2[l], w_cv1[l], b_cv1[l], w_cv2[l])
        moe_w = (w_rg[l], b_rg[l], w_re[l], b_re[l], w_gate[l], w_up[l], w_down[l])
        mix_p, st_p = mixer_prompt(rms_norm(h_p, g_mix[l]), w_in[l], conv_w, cmp_w, w_nsa_out[l], w_o[l])
        h_p = h_p + mix_p
        h_p = h_p + hier_moe(rms_norm(h_p, g_ffn[l]), *moe_w)
        mix_s, st_s = mixer_sample(rms_norm(h_s, g_mix[l]), cache_k_cmp[l], cache_v_cmp[l], cache_k_sel[l],
                                   cache_v_sel[l], state_k_win[l], state_v_win[l], state_conv[l], page_table,
                                   w_in[l], conv_w, cmp_w, w_nsa_out[l], w_o[l])
        h_s = h_s + mix_s
        h_s = h_s + hier_moe(rms_norm(h_s, g_ffn[l]), *moe_w)
        per_layer.append(st_p + st_s)
    (k_cmp_p, v_cmp_p, k_sel_p, v_sel_p, k_win_p, v_win_p, conv_p,
     k_cmp_s, v_cmp_s, k_sel_s, v_sel_s, k_win_s, v_win_s, conv_s) = [jnp.stack(s) for s in zip(*per_layer)]
    y_prompt = rms_norm(h_p, g_final)
    y_sample = rms_norm(h_s, g_final)
    return (y_prompt, y_sample, k_cmp_p, v_cmp_p, k_sel_p, v_sel_p, k_win_p, v_win_p, conv_p,
            k_cmp_s, v_cmp_s, k_sel_s, v_sel_s, k_win_s, v_win_s, conv_s)
```

```python
import functools

import numpy as np
import jax
import jax.numpy as jnp
from jax import lax
from jax.experimental import pallas as pl
from jax.experimental.pallas import tpu as pltpu

D_MODEL = 1024
C_CONV = 512
CONV_K = 31
N_HEADS = 8
N_KV = 2
HEAD_DIM = 64
HPG = N_HEADS // N_KV
CMP_BLOCK = 32
CMP_STRIDE = 16
SEL_BLOCK = 64
N_SEL = 16
WINDOW = 512
PAGE_SIZE = 128
ROPE_THETA = 10000.0
FORCE_BONUS = 1000.0
NEG_INF = -1e30
N_GROUPS = 4
EXP_PER_GROUP = 8
N_EXPERTS = N_GROUPS * EXP_PER_GROUP
D_EXPERT = 128
EPS = 1e-6
KV_W = N_KV * HEAD_DIM
Q_W = N_HEADS * HEAD_DIM
SCALE = HEAD_DIM ** -0.5

LANES = 128
SUBLANES = 8
VMEM_LIMIT = 56 * 1024 * 1024

BF16 = jnp.bfloat16
F32 = jnp.float32

_OFF_U = 0
_OFF_Q = 2 * C_CONV
_OFF_KV = _OFF_Q + Q_W
_OFF_NG = _OFF_KV + 6 * KV_W
_OFF_MG = _OFF_NG + 3 * N_HEADS
_IN_COLS = _OFF_MG + 2 * D_MODEL


def _cparams(sem):
    return pltpu.CompilerParams(dimension_semantics=sem, vmem_limit_bytes=VMEM_LIMIT)


def _dot(a, b):
    return jnp.dot(a, b, preferred_element_type=F32)


def _dot_nt(a, b):
    return lax.dot_general(a, b, (((1,), (1,)), ((), ())), preferred_element_type=F32)


def _rms(x, g):
    ms = jnp.mean(x * x, axis=-1, keepdims=True)
    return x * lax.rsqrt(ms + EPS) * g


def _split3(x):
    hi = x.astype(BF16)
    r1 = x - hi.astype(F32)
    mid = r1.astype(BF16)
    lo = (r1 - mid.astype(F32)).astype(BF16)
    return hi, mid, lo


def _rope128(x, cos, sin_signed, first_half):
    swapped = jnp.where(first_half, pltpu.roll(x, LANES - HEAD_DIM // 2, 1), pltpu.roll(x, HEAD_DIM // 2, 1))
    return x * cos + swapped * sin_signed


def _proj_in_body(x_ref, g_ref, wu_ref, wq_ref, wkv_ref, wng_ref, cos_ref, sin_ref,
                  a_ref, q_ref, qr_ref, kc_ref, vc_ref, ks_ref, vs_ref, kw_ref, vw_ref, ng_ref, kvb_ref):
    xn = _rms(x_ref[...], g_ref[...]).astype(BF16)
    cos = cos_ref[...]
    sin = sin_ref[...]
    lane = lax.broadcasted_iota(jnp.int32, cos.shape, 1)
    first_half = (lane % HEAD_DIM) < (HEAD_DIM // 2)

    u = _dot(xn, wu_ref[...])
    a_ref[...] = u[:, :C_CONV] * jax.nn.sigmoid(u[:, C_CONV:])

    q = _dot(xn, wq_ref[...])
    q_ref[...] = q.astype(BF16)
    for c in range(Q_W // LANES):
        qc = q[:, c * LANES:(c + 1) * LANES]
        qr_ref[:, c * LANES:(c + 1) * LANES] = _rope128(qc, cos, sin, first_half).astype(BF16)

    kv = _dot(xn, wkv_ref[...])
    kc = kv[:, 0 * KV_W:1 * KV_W]
    vc = kv[:, 1 * KV_W:2 * KV_W]
    ks = _rope128(kv[:, 2 * KV_W:3 * KV_W], cos, sin, first_half)
    vs = kv[:, 3 * KV_W:4 * KV_W]
    kw = _rope128(kv[:, 4 * KV_W:5 * KV_W], cos, sin, first_half)
    vw = kv[:, 5 * KV_W:6 * KV_W]
    kc_ref[...] = kc
    vc_ref[...] = vc
    ks_ref[...] = ks
    vs_ref[...] = vs
    kw_ref[...] = kw
    vw_ref[...] = vw
    for i, piece in enumerate((kc, vc, ks, vs, kw, vw)):
        kvb_ref[:, i * KV_W:(i + 1) * KV_W] = piece.astype(BF16)

    ng_ref[...] = jax.nn.sigmoid(_dot(xn, wng_ref[...]))


def _proj_in(x2d, g_mix, w_in, pos_period, tm):
    n = x2d.shape[0]
    assert n % tm == 0
    p = pos_period.shape[0]
    if p < tm:
        assert tm % p == 0
        pos_period = jnp.tile(pos_period, tm // p)
        p = tm
    assert p % tm == 0
    half = HEAD_DIM // 2
    inv_freq = ROPE_THETA ** (-jnp.arange(half, dtype=F32) * 2.0 / HEAD_DIM)
    ang = pos_period.astype(F32)[:, None] * inv_freq[None, :]
    cos = jnp.tile(jnp.cos(ang), (1, LANES // half))
    sin = jnp.sin(ang)
    sin = jnp.tile(jnp.concatenate([-sin, sin], axis=1), (1, LANES // HEAD_DIM))

    wb = w_in.astype(BF16)
    wu = wb[:, _OFF_U:_OFF_Q]
    wq = wb[:, _OFF_Q:_OFF_KV]
    wkv = wb[:, _OFF_KV:_OFF_NG]
    wng = jnp.pad(wb[:, _OFF_NG:_OFF_MG], ((0, 0), (0, LANES - 3 * N_HEADS)))

    nper = p // tm
    row = lambda w: pl.BlockSpec((tm, w), lambda i: (i, 0))
    full = lambda a: pl.BlockSpec(a.shape, lambda i: (0, 0))
    tab = pl.BlockSpec((tm, LANES), lambda i: (i % nper, 0))
    outs = [
        jax.ShapeDtypeStruct((n, C_CONV), F32),
        jax.ShapeDtypeStruct((n, Q_W), BF16),
        jax.ShapeDtypeStruct((n, Q_W), BF16),
    ] + [jax.ShapeDtypeStruct((n, KV_W), F32)] * 6 + [
        jax.ShapeDtypeStruct((n, LANES), F32),
        jax.ShapeDtypeStruct((n, 6 * KV_W), BF16),
    ]
    out_specs = [row(C_CONV), row(Q_W), row(Q_W)] + [row(KV_W)] * 6 + [row(LANES), row(6 * KV_W)]
    return pl.pallas_call(
        _proj_in_body,
        grid=(n // tm,),
        in_specs=[row(D_MODEL), full(g_mix), full(wu), full(wq), full(wkv), full(wng), tab, tab],
        out_specs=out_specs,
        out_shape=outs,
        compiler_params=_cparams(("parallel",)),
        name="proj_in",
    )(x2d, g_mix, wu, wq, wkv, wng, cos, sin)


_HALO = 32


def _conv_body(a_ref, halo_ref, wdw_ref, bdw_ref, ng_ref, nb_ref, wout_ref, y_ref, full_ref, c_ref,
               *, bb, tt, halo_from_prev_tile):
    if halo_from_prev_tile:
        first = pl.program_id(1) == 0

        @pl.when(first)
        def _():
            full_ref[:, 0:_HALO, :] = jnp.zeros((bb, _HALO, C_CONV), F32)

        @pl.when(jnp.logical_not(first))
        def _():
            full_ref[:, 0:_HALO, :] = halo_ref[...]
    else:
        full_ref[:, 0:_HALO, :] = halo_ref[...]
    full_ref[:, _HALO:_HALO + tt, :] = a_ref[...]

    base = _HALO - (CONV_K - 1)
    rc = min(tt, 64)
    for s in range(bb):
        for r0 in range(0, tt, rc):
            for l0 in range(0, C_CONV, LANES):
                acc = jnp.zeros((rc, LANES), F32)
                for k in range(CONV_K):
                    rows = slice(base + r0 + k, base + r0 + k + rc)
                    acc = acc + full_ref[s, rows, l0:l0 + LANES] * wdw_ref[k:k + 1, l0:l0 + LANES]
                c_ref[s * tt + r0:s * tt + r0 + rc, l0:l0 + LANES] = acc + bdw_ref[:, l0:l0 + LANES]

    c = c_ref[...]
    mu = jnp.mean(c, axis=-1, keepdims=True)
    d = c - mu
    var = jnp.mean(d * d, axis=-1, keepdims=True)
    cn = d * lax.rsqrt(var + EPS) * ng_ref[...] + nb_ref[...]
    act = (cn * jax.nn.sigmoid(cn)).astype(BF16)
    y_ref[...] = _dot(act, wout_ref[...]).reshape(bb, tt, D_MODEL)


def _conv(a3, hist, w_dw, b_dw, n_g, n_b, w_out, bb, tt):
    b, t, _ = a3.shape
    assert t % tt == 0 and b % bb == 0 and tt % SUBLANES == 0
    wdw = jnp.pad(w_dw, ((0, 1), (0, 0)))
    wout = w_out.astype(BF16)
    from_prev = hist is None
    if from_prev:
        assert tt % _HALO == 0
        ratio = tt // _HALO
        halo_arr = a3
        halo_spec = pl.BlockSpec((bb, _HALO, C_CONV), lambda bi, i: (bi, jnp.maximum(i * ratio - 1, 0), 0))
    else:
        assert t == tt
        halo_arr = hist
        halo_spec = pl.BlockSpec((bb, _HALO, C_CONV), lambda bi, i: (bi, 0, 0))
    vec = pl.BlockSpec((1, C_CONV), lambda bi, i: (0, 0))
    return pl.pallas_call(
        functools.partial(_conv_body, bb=bb, tt=tt, halo_from_prev_tile=from_prev),
        grid=(b // bb, t // tt),
        in_specs=[
            pl.BlockSpec((bb, tt, C_CONV), lambda bi, i: (bi, i, 0)),
            halo_spec,
            pl.BlockSpec(wdw.shape, lambda bi, i: (0, 0)),
            vec, vec, vec,
            pl.BlockSpec(wout.shape, lambda bi, i: (0, 0)),
        ],
        out_specs=pl.BlockSpec((bb, tt, D_MODEL), lambda bi, i: (bi, i, 0)),
        out_shape=jax.ShapeDtypeStruct((b, t, D_MODEL), F32),
        scratch_shapes=[pltpu.VMEM((bb, _HALO + tt, C_CONV), F32), pltpu.VMEM((bb * tt, C_CONV), F32)],
        compiler_params=_cparams(("parallel", "arbitrary")),
        name="conv",
    )(a3, halo_arr, wdw, b_dw.reshape(1, -1), n_g.reshape(1, -1), n_b.reshape(1, -1), wout)


def _compress_weights(w1, b1, w2):
    eye = jnp.eye(N_KV, dtype=F32)
    big = jnp.einsum("gh,lde->lgdhe", eye, w1).reshape(CMP_BLOCK, KV_W, KV_W)
    wa = big[:CMP_STRIDE].reshape(CMP_STRIDE * KV_W, KV_W)
    wb = big[CMP_STRIDE:].reshape(CMP_STRIDE * KV_W, KV_W)
    wcat = jnp.concatenate([wa, wb], axis=1).astype(BF16)
    w2bd = jnp.einsum("gh,ef->gehf", eye, w2).reshape(KV_W, KV_W).astype(BF16)
    b1t = jnp.tile(b1, N_KV).reshape(1, KV_W)
    return wcat, b1t, w2bd


def _compress_rows(y_bf, wcat_ref, b1_ref, w2_ref):
    s = y_bf.shape[0]
    p = _dot(y_bf, wcat_ref[...])
    pre = p[:, :KV_W] + pltpu.roll(p[:, KV_W:], s - 1, 0) + b1_ref[...]
    hid = (pre * jax.nn.sigmoid(pre)).astype(BF16)
    return _dot(hid, w2_ref[...])


def _compress_prompt_body(k_ref, v_ref, wk_ref, bk_ref, wk2_ref, wv_ref, bv_ref, wv2_ref, kc_ref, vc_ref):
    kc_ref[0] = _compress_rows(k_ref[0], wk_ref, bk_ref, wk2_ref).astype(BF16)
    vc_ref[0] = _compress_rows(v_ref[0], wv_ref, bv_ref, wv2_ref).astype(BF16)


def _compress_prompt(kvb, b, t, cmp_w):
    s = t // CMP_STRIDE
    feat = CMP_STRIDE * KV_W
    k_rows = kvb[:, 0:KV_W].reshape(b, s, feat)
    v_rows = kvb[:, KV_W:2 * KV_W].reshape(b, s, feat)
    wk, bk, wk2 = _compress_weights(*cmp_w[:3])
    wv, bv, wv2 = _compress_weights(*cmp_w[3:])
    rows = pl.BlockSpec((1, s, feat), lambda i: (i, 0, 0))
    full = lambda a: pl.BlockSpec(a.shape, lambda i: (0, 0))
    out = pl.BlockSpec((1, s, KV_W), lambda i: (i, 0, 0))
    return pl.pallas_call(
        _compress_prompt_body,
        grid=(b,),
        in_specs=[rows, rows, full(wk), full(bk), full(wk2), full(wv), full(bv), full(wv2)],
        out_specs=[out, out],
        out_shape=[jax.ShapeDtypeStruct((b, s, KV_W), BF16)] * 2,
        compiler_params=_cparams(("parallel",)),
        name="compress_prompt",
    )(k_rows, v_rows, wk, bk, wk2, wv, bv, wv2)


def _page_copy(pool_ref, buf_ref, sem_ref, pt_ref, seq, slot, j, n_pages):
    page = pt_ref[seq * n_pages + j]
    return pltpu.make_async_copy(pool_ref.at[page], buf_ref.at[slot, j], sem_ref.at[slot])


def _gather_start(pools, bufs, sems, pt_ref, seq, slot, n_pages):
    def body(j, carry):
        for pool_ref, buf_ref, sem_ref in zip(pools, bufs, sems):
            _page_copy(pool_ref, buf_ref, sem_ref, pt_ref, seq, slot, j, n_pages).start()
        return carry
    lax.fori_loop(0, n_pages, body, 0)


def _gather_wait(pools, bufs, sems, pt_ref, seq, slot, n_pages):
    def body(j, carry):
        for pool_ref, buf_ref, sem_ref in zip(pools, bufs, sems):
            _page_copy(pool_ref, buf_ref, sem_ref, pt_ref, seq, slot, j, n_pages).wait()
        return carry
    lax.fori_loop(0, n_pages, body, 0)


def _paged_prologue(pools, bufs, sems, pt_ref, n_pages):
    i = pl.program_id(0)
    n = pl.num_programs(0)
    slot = i % 2

    @pl.when(i == 0)
    def _():
        _gather_start(pools, bufs, sems, pt_ref, i, slot, n_pages)

    @pl.when(i + 1 < n)
    def _():
        _gather_start(pools, bufs, sems, pt_ref, i + 1, 1 - slot, n_pages)

    _gather_wait(pools, bufs, sems, pt_ref, i, slot, n_pages)
    return slot


def _compress_sample_body(pt_ref, kpool_ref, vpool_ref, wk_ref, bk_ref, wk2_ref, wv_ref, bv_ref, wv2_ref,
                          kc_ref, vc_ref, kbuf, vbuf, ksem, vsem, *, n_pages):
    slot = _paged_prologue((kpool_ref, vpool_ref), (kbuf, vbuf), (ksem, vsem), pt_ref, n_pages)
    spp = PAGE_SIZE // CMP_STRIDE
    feat = CMP_STRIDE * KV_W
    yk = kbuf[slot].reshape(n_pages * spp, feat).astype(BF16)
    kc_ref[0] = _compress_rows(yk, wk_ref, bk_ref, wk2_ref).astype(BF16)
    yv = vbuf[slot].reshape(n_pages * spp, feat).astype(BF16)
    vc_ref[0] = _compress_rows(yv, wv_ref, bv_ref, wv2_ref).astype(BF16)


def _compress_sample(pool_k, pool_v, page_table, cmp_w):
    b, n_pages = page_table.shape
    n_phys = pool_k.shape[0]
    spp = PAGE_SIZE // CMP_STRIDE
    feat = CMP_STRIDE * KV_W
    s = n_pages * spp
    pk = pool_k.reshape(n_phys, spp, feat)
    pv = pool_v.reshape(n_phys, spp, feat)
    wk, bk, wk2 = _compress_weights(*cmp_w[:3])
    wv, bv, wv2 = _compress_weights(*cmp_w[3:])
    full = lambda a: pl.BlockSpec(a.shape, lambda i, pt: (0, 0))
    out = pl.BlockSpec((1, s, KV_W), lambda i, pt: (i, 0, 0))
    anyspec = pl.BlockSpec(memory_space=pl.ANY)
    grid_spec = pltpu.PrefetchScalarGridSpec(
        num_scalar_prefetch=1,
        grid=(b,),
        in_specs=[anyspec, anyspec, full(wk), full(bk), full(wk2), full(wv), full(bv), full(wv2)],
        out_specs=[out, out],
        scratch_shapes=[
            pltpu.VMEM((2, n_pages, spp, feat), F32),
            pltpu.VMEM((2, n_pages, spp, feat), F32),
            pltpu.SemaphoreType.DMA((2,)),
            pltpu.SemaphoreType.DMA((2,)),
        ],
    )
    return pl.pallas_call(
        functools.partial(_compress_sample_body, n_pages=n_pages),
        grid_spec=grid_spec,
        out_shape=[jax.ShapeDtypeStruct((b, s, KV_W), BF16)] * 2,
        compiler_params=_cparams(("arbitrary",)),
        name="compress_sample",
    )(page_table.reshape(-1), pk, pv, wk, bk, wk2, wv, bv, wv2)


def _overlap_matrix(nc_pad, ns_pad):
    i = np.arange(nc_pad)[:, None]
    j = np.arange(ns_pad)[None, :]
    ov = (i * CMP_STRIDE < (j + 1) * SEL_BLOCK) & (i * CMP_STRIDE + CMP_BLOCK > j * SEL_BLOCK)
    return jnp.asarray(ov.astype(np.float32), dtype=BF16)


def _softmax_rows(s):
    m = jnp.max(s, axis=-1, keepdims=True)
    e = jnp.exp(s - m)
    return e / jnp.sum(e, axis=-1, keepdims=True)


def _top_k_mask(imp, k):
    ns = imp.shape[1]
    col = lax.broadcasted_iota(jnp.int32, imp.shape, 1)
    sel = jnp.zeros(imp.shape, F32)
    work = imp
    for _ in range(k):
        m = jnp.max(work, axis=-1, keepdims=True)
        idx = jnp.min(jnp.where(work == m, col, ns), axis=-1, keepdims=True)
        hit = col == idx
        sel = jnp.where(hit, 1.0, sel)
        work = jnp.where(hit, -3e38, work)
    return sel


def _stack_heads(q, g):
    return jnp.concatenate([q[:, (g * HPG + h) * HEAD_DIM:(g * HPG + h + 1) * HEAD_DIM] for h in range(HPG)], axis=0)


def _cmp_branch(qg, kc_g, vc_g, t_col, ov, nq):
    nc = kc_g.shape[0]
    s = _dot_nt(qg, kc_g) * SCALE
    c_end = lax.broadcasted_iota(jnp.int32, (nq, nc), 1) * CMP_STRIDE + (CMP_BLOCK - 1)
    ok = c_end <= t_col
    ok4 = jnp.concatenate([ok] * HPG, axis=0)
    s = jnp.where(ok4, s, NEG_INF)
    p = _softmax_rows(s)
    any_ok = (t_col >= CMP_BLOCK - 1).astype(F32)
    p = p * jnp.concatenate([any_ok] * HPG, axis=0)
    o_c = _dot(p.astype(BF16), vc_g)
    p_sum = p[0:nq] + p[nq:2 * nq] + p[2 * nq:3 * nq] + p[3 * nq:4 * nq]
    hi, mid, lo = _split3(p_sum)
    imp = _dot(hi, ov) + _dot(mid, ov) + _dot(lo, ov)
    return o_c, imp


def _select(imp, t_col):
    ns = imp.shape[1]
    j = lax.broadcasted_iota(jnp.int32, imp.shape, 1)
    cur = t_col // SEL_BLOCK
    forced = (j == 0) | (j == cur) | (j == cur - 1)
    blk_ok = j * SEL_BLOCK <= t_col
    imp = jnp.where(blk_ok, imp + FORCE_BONUS * forced.astype(F32), NEG_INF)
    return _top_k_mask(imp, min(N_SEL, ns))


def _gate_cols(ng, g, col):
    return jnp.concatenate([ng[:, (g * HPG + h) * 3 + col:(g * HPG + h) * 3 + col + 1] for h in range(HPG)], axis=0)


def _write_heads(o_ref, o, g, nq):
    for h in range(HPG):
        c0 = (g * HPG + h) * HEAD_DIM
        o_ref[0, :, c0:c0 + HEAD_DIM] = o[h * nq:(h + 1) * nq].astype(o_ref.dtype)


def _nsa_prompt_body(q_ref, qr_ref, ng_ref, kc_ref, vc_ref, kvb_ref, ov_ref, e_ref, o_ref, *, qb, kc_len, t_len):
    i = pl.program_id(1)
    start = i * qb
    t_col = start + lax.broadcasted_iota(jnp.int32, (qb, 1), 0)
    q = q_ref[0]
    qr = qr_ref[0]
    ng = ng_ref[0]
    n_chunks = (start + qb + kc_len - 1) // kc_len
    w_len = WINDOW + qb
    w_start = pl.multiple_of(jnp.maximum(start - WINDOW, 0), qb)

    for g in range(N_KV):
        lanes = slice(g * HEAD_DIM, (g + 1) * HEAD_DIM)
        o_c, imp = _cmp_branch(_stack_heads(q, g), kc_ref[0][:, lanes], vc_ref[0][:, lanes], t_col, ov_ref[...], qb)
        sel = _select(imp, t_col).astype(BF16)

        qrg = _stack_heads(qr, g)

        def chunk(c, carry):
            m, l, acc = carry
            k0 = pl.multiple_of(c * kc_len, kc_len)
            kch = kvb_ref[0, pl.ds(k0, kc_len), 2 * KV_W + g * HEAD_DIM:2 * KV_W + (g + 1) * HEAD_DIM]
            vch = kvb_ref[0, pl.ds(k0, kc_len), 3 * KV_W + g * HEAD_DIM:3 * KV_W + (g + 1) * HEAD_DIM]
            key = k0 + lax.broadcasted_iota(jnp.int32, (qb, kc_len), 1)
            ok = (_dot(sel, e_ref[c]) > 0.5) & (key <= t_col)
            ok4 = jnp.concatenate([ok] * HPG, axis=0)
            s = jnp.where(ok4, _dot_nt(qrg, kch) * SCALE, NEG_INF)
            m_new = jnp.maximum(m, jnp.max(s, axis=-1, keepdims=True))
            alpha = jnp.exp(m - m_new)
            p = jnp.exp(s - m_new)
            l = alpha * l + jnp.sum(p, axis=-1, keepdims=True)
            acc = alpha * acc + _dot(p.astype(BF16), vch)
            return m_new, l, acc

        init = (jnp.full((HPG * qb, 1), NEG_INF, F32), jnp.zeros((HPG * qb, 1), F32), jnp.zeros((HPG * qb, HEAD_DIM), F32))
        _, l, acc = lax.fori_loop(0, n_chunks, chunk, init)
        o_s = acc / l

        kwc = kvb_ref[0, pl.ds(w_start, w_len), 4 * KV_W + g * HEAD_DIM:4 * KV_W + (g + 1) * HEAD_DIM]
        vwc = kvb_ref[0, pl.ds(w_start, w_len), 5 * KV_W + g * HEAD_DIM:5 * KV_W + (g + 1) * HEAD_DIM]
        key = w_start + lax.broadcasted_iota(jnp.int32, (qb, w_len), 1)
        ok = (key <= t_col) & (t_col - key < WINDOW)
        ok4 = jnp.concatenate([ok] * HPG, axis=0)
        p_w = _softmax_rows(jnp.where(ok4, _dot_nt(qrg, kwc) * SCALE, NEG_INF))
        o_w = _dot(p_w.astype(BF16), vwc)

        o = _gate_cols(ng, g, 0) * o_c + _gate_cols(ng, g, 1) * o_s + _gate_cols(ng, g, 2) * o_w
        _write_heads(o_ref, o, g, qb)


def _nsa_prompt(q, qr, ng, kc, vc, kvb, b, t, qb, kc_len):
    assert t % kc_len == 0 and kc_len % qb == 0 and t >= WINDOW + qb and WINDOW % qb == 0
    ns = t // SEL_BLOCK
    nc_pad = t // CMP_STRIDE
    ov = _overlap_matrix(nc_pad, ns)
    blk = np.arange(t) // SEL_BLOCK
    e = (np.arange(ns)[:, None] == blk[None, :]).astype(np.float32)
    e3 = jnp.asarray(e.reshape(ns, t // kc_len, kc_len).transpose(1, 0, 2), dtype=BF16)
    tile = lambda w: pl.BlockSpec((1, qb, w), lambda bi, i: (bi, i, 0))
    per_b = lambda rows, w: pl.BlockSpec((1, rows, w), lambda bi, i: (bi, 0, 0))
    return pl.pallas_call(
        functools.partial(_nsa_prompt_body, qb=qb, kc_len=kc_len, t_len=t),
        grid=(b, t // qb),
        in_specs=[
            tile(Q_W), tile(Q_W), tile(LANES),
            per_b(nc_pad, KV_W), per_b(nc_pad, KV_W), per_b(t, 6 * KV_W),
            pl.BlockSpec(ov.shape, lambda bi, i: (0, 0)),
            pl.BlockSpec(e3.shape, lambda bi, i: (0, 0, 0)),
        ],
        out_specs=tile(Q_W),
        out_shape=jax.ShapeDtypeStruct((b, t, Q_W), BF16),
        compiler_params=_cparams(("parallel", "arbitrary")),
        name="nsa_prompt",
    )(q.reshape(b, t, Q_W), qr.reshape(b, t, Q_W), ng.reshape(b, t, LANES), kc, vc, kvb.reshape(b, t, 6 * KV_W), ov, e3)


def _nsa_sample_body(pt_ref, q_ref, qr_ref, ng_ref, kc_ref, vc_ref, kvn_ref, kwin_ref, vwin_ref, kpool_ref, vpool_ref,
                     ov_ref, e_ref, o_ref, kbuf, vbuf, ksem, vsem, *, n_pages, tq):
    slot = _paged_prologue((kpool_ref, vpool_ref), (kbuf, vbuf), (ksem, vsem), pt_ref, n_pages)
    past = n_pages * PAGE_SIZE
    t_col = past + lax.broadcasted_iota(jnp.int32, (tq, 1), 0)
    q = q_ref[0]
    qr = qr_ref[0]
    ng = ng_ref[0]
    kvn = kvn_ref[0]
    ks_all = kbuf[slot].reshape(past, KV_W).astype(BF16)
    vs_all = vbuf[slot].reshape(past, KV_W).astype(BF16)
    wb = kwin_ref.shape[1]
    kw_past = kwin_ref[0].astype(BF16)
    vw_past = vwin_ref[0].astype(BF16)
    ns_past = past // SEL_BLOCK

    for g in range(N_KV):
        lanes = slice(g * HEAD_DIM, (g + 1) * HEAD_DIM)
        o_c, imp = _cmp_branch(_stack_heads(q, g), kc_ref[0][:, lanes], vc_ref[0][:, lanes], t_col, ov_ref[...], tq)
        sel = _select(imp, t_col)
        qrg = _stack_heads(qr, g)

        ok_past = _dot(sel.astype(BF16), e_ref[...]) > 0.5
        ok4 = jnp.concatenate([ok_past] * HPG, axis=0)
        s_past = jnp.where(ok4, _dot_nt(qrg, ks_all[:, lanes]) * SCALE, NEG_INF)
        kn = kvn[:, 2 * KV_W + g * HEAD_DIM:2 * KV_W + (g + 1) * HEAD_DIM]
        vn = kvn[:, 3 * KV_W + g * HEAD_DIM:3 * KV_W + (g + 1) * HEAD_DIM]
        new_sel = sel[:, ns_past:ns_past + 1] > 0.5
        key_new = past + lax.broadcasted_iota(jnp.int32, (tq, tq), 1)
        ok_new = new_sel & (key_new <= t_col)
        ok_new4 = jnp.concatenate([ok_new] * HPG, axis=0)
        s_new = jnp.where(ok_new4, _dot_nt(qrg, kn) * SCALE, NEG_INF)
        m = jnp.maximum(jnp.max(s_past, axis=-1, keepdims=True), jnp.max(s_new, axis=-1, keepdims=True))
        p_past = jnp.exp(s_past - m)
        p_new = jnp.exp(s_new - m)
        l = jnp.sum(p_past, axis=-1, keepdims=True) + jnp.sum(p_new, axis=-1, keepdims=True)
        o_s = (_dot(p_past.astype(BF16), vs_all[:, lanes]) + _dot(p_new.astype(BF16), vn)) / l

        kwn = kvn[:, 4 * KV_W + g * HEAD_DIM:4 * KV_W + (g + 1) * HEAD_DIM]
        vwn = kvn[:, 5 * KV_W + g * HEAD_DIM:5 * KV_W + (g + 1) * HEAD_DIM]
        pos_p = (past - wb) + lax.broadcasted_iota(jnp.int32, (tq, wb), 1)
        ok_p = (pos_p <= t_col) & (t_col - pos_p < WINDOW) & (pos_p >= 0)
        ok_p4 = jnp.concatenate([ok_p] * HPG, axis=0)
        sw_p = jnp.where(ok_p4, _dot_nt(qrg, kw_past[:, lanes]) * SCALE, NEG_INF)
        ok_n = key_new <= t_col
        ok_n4 = jnp.concatenate([ok_n] * HPG, axis=0)
        sw_n = jnp.where(ok_n4, _dot_nt(qrg, kwn) * SCALE, NEG_INF)
        mw = jnp.maximum(jnp.max(sw_p, axis=-1, keepdims=True), jnp.max(sw_n, axis=-1, keepdims=True))
        pw_p = jnp.exp(sw_p - mw)
        pw_n = jnp.exp(sw_n - mw)
        lw = jnp.sum(pw_p, axis=-1, keepdims=True) + jnp.sum(pw_n, axis=-1, keepdims=True)
        o_w = (_dot(pw_p.astype(BF16), vw_past[:, lanes]) + _dot(pw_n.astype(BF16), vwn)) / lw

        o = _gate_cols(ng, g, 0) * o_c + _gate_cols(ng, g, 1) * o_s + _gate_cols(ng, g, 2) * o_w
        _write_heads(o_ref, o, g, tq)


def _nsa_sample(q, qr, ng, kc, vc, kvb, k_win, v_win, pool_k, pool_v, page_table, tq):
    b, n_pages = page_table.shape
    past = n_pages * PAGE_SIZE
    assert tq < CMP_STRIDE and tq <= SEL_BLOCK and past % SEL_BLOCK == 0
    n_phys = pool_k.shape[0]
    wb = k_win.shape[1]
    ns_past = past // SEL_BLOCK
    ns_pad = ((ns_past + 1 + LANES - 1) // LANES) * LANES
    nc_pad = past // CMP_STRIDE
    ov = _overlap_matrix(nc_pad, ns_pad)
    blk = np.arange(past) // SEL_BLOCK
    e = jnp.asarray((np.arange(ns_pad)[:, None] == blk[None, :]).astype(np.float32), dtype=BF16)
    per_b = lambda rows, w: pl.BlockSpec((1, rows, w), lambda i, pt: (i, 0, 0))
    anyspec = pl.BlockSpec(memory_space=pl.ANY)
    grid_spec = pltpu.PrefetchScalarGridSpec(
        num_scalar_prefetch=1,
        grid=(b,),
        in_specs=[
            per_b(tq, Q_W), per_b(tq, Q_W), per_b(tq, LANES),
            per_b(nc_pad, KV_W), per_b(nc_pad, KV_W), per_b(tq, 6 * KV_W),
            per_b(wb, KV_W), per_b(wb, KV_W),
            anyspec, anyspec,
            pl.BlockSpec(ov.shape, lambda i, pt: (0, 0)),
            pl.BlockSpec(e.shape, lambda i, pt: (0, 0)),
        ],
        out_specs=per_b(tq, Q_W),
        scratch_shapes=[
            pltpu.VMEM((2, n_pages, PAGE_SIZE, KV_W), F32),
            pltpu.VMEM((2, n_pages, PAGE_SIZE, KV_W), F32),
            pltpu.SemaphoreType.DMA((2,)),
            pltpu.SemaphoreType.DMA((2,)),
        ],
    )
    return pl.pallas_call(
        functools.partial(_nsa_sample_body, n_pages=n_pages, tq=tq),
        grid_spec=grid_spec,
        out_shape=jax.ShapeDtypeStruct((b, tq, Q_W), BF16),
        compiler_params=_cparams(("arbitrary",)),
        name="nsa_sample",
    )(page_table.reshape(-1), q.reshape(b, tq, Q_W), qr.reshape(b, tq, Q_W), ng.reshape(b, tq, LANES), kc, vc,
      kvb.reshape(b, tq, 6 * KV_W), k_win.reshape(b, wb, KV_W), v_win.reshape(b, wb, KV_W),
      pool_k.reshape(n_phys, PAGE_SIZE, KV_W), pool_v.reshape(n_phys, PAGE_SIZE, KV_W), ov, e)


def _merge_body(x_ref, g_ref, yc_ref, o_ref, wmg_ref, wn_ref, wo_ref, h_ref):
    x = x_ref[...]
    xn = _rms(x, g_ref[...]).astype(BF16)
    mg = jax.nn.sigmoid(_dot(xn, wmg_ref[...]))
    y_nsa = _dot(o_ref[...], wn_ref[...])
    mix = mg[:, :D_MODEL] * yc_ref[...] + mg[:, D_MODEL:] * y_nsa
    h_ref[...] = x + _dot(mix.astype(BF16), wo_ref[...])


def _merge(x2d, g_mix, y_conv, o, w_in, w_nsa_out, w_o, tm):
    n = x2d.shape[0]
    assert n % tm == 0
    wmg = w_in[:, _OFF_MG:_IN_COLS].astype(BF16)
    wn = w_nsa_out.astype(BF16)
    wo = w_o.astype(BF16)
    row = lambda w: pl.BlockSpec((tm, w), lambda i: (i, 0))
    full = lambda a: pl.BlockSpec(a.shape, lambda i: (0, 0))
    return pl.pallas_call(
        _merge_body,
        grid=(n // tm,),
        in_specs=[row(D_MODEL), full(g_mix), row(D_MODEL), row(Q_W), full(wmg), full(wn), full(wo)],
        out_specs=row(D_MODEL),
        out_shape=jax.ShapeDtypeStruct((n, D_MODEL), F32),
        compiler_params=_cparams(("parallel",)),
        name="merge",
    )(x2d, g_mix, y_conv, o, wmg, wn, wo)


_EXP_CHUNK = 2


def _moe_body(h_ref, gf_ref, wr_hi_ref, wr_mid_ref, wr_lo_ref, br_ref, wg_ref, wu_ref, wd_ref, gfin_ref, y_ref):
    h = h_ref[...]
    xn = _rms(h, gf_ref[...])
    tm = h.shape[0]

    x_hi, x_mid, x_lo = _split3(xn)
    w_hi, w_mid, w_lo = wr_hi_ref[...], wr_mid_ref[...], wr_lo_ref[...]
    logits = (_dot(x_hi, w_hi) + (_dot(x_hi, w_mid) + _dot(x_mid, w_hi))
              + (_dot(x_hi, w_lo) + _dot(x_mid, w_mid) + _dot(x_lo, w_hi))) + br_ref[...]
    lane = lax.broadcasted_iota(jnp.int32, logits.shape, 1)
    is_grp = lane < N_GROUPS
    lg = jnp.where(is_grp, logits, -jnp.inf)
    lg_max = jnp.max(lg, axis=-1, keepdims=True)
    g_star = jnp.min(jnp.where(lg == lg_max, lane, LANES), axis=-1, keepdims=True)
    p_grp = 1.0 / jnp.sum(jnp.where(is_grp, jnp.exp(lg - lg_max), 0.0), axis=-1, keepdims=True)
    eid = lane - N_GROUPS
    in_grp = (eid >= g_star * EXP_PER_GROUP) & (eid < (g_star + 1) * EXP_PER_GROUP)
    le = jnp.where(in_grp, logits, -jnp.inf)
    v1 = jnp.max(le, axis=-1, keepdims=True)
    i1 = jnp.min(jnp.where(le == v1, lane, LANES), axis=-1, keepdims=True)
    le2 = jnp.where(lane == i1, -jnp.inf, le)
    v2 = jnp.max(le2, axis=-1, keepdims=True)
    i2 = jnp.min(jnp.where(le2 == v2, lane, LANES), axis=-1, keepdims=True)
    e2 = jnp.exp(v2 - v1)
    w1 = p_grp / (1.0 + e2)
    w2 = p_grp * e2 / (1.0 + e2)
    comb = jnp.where(lane == i1, w1, 0.0) + jnp.where(lane == i2, w2, 0.0)

    xb = xn.astype(BF16)
    cw = _EXP_CHUNK * D_EXPERT
    y = jnp.zeros((tm, D_MODEL), F32)
    for c in range(N_EXPERTS // _EXP_CHUNK):
        hid = _dot(xb, wg_ref[:, c * cw:(c + 1) * cw])
        hid = hid * jax.nn.sigmoid(hid) * _dot(xb, wu_ref[:, c * cw:(c + 1) * cw])
        parts = []
        for j in range(_EXP_CHUNK):
            e_lane = N_GROUPS + c * _EXP_CHUNK + j
            parts.append(hid[:, j * D_EXPERT:(j + 1) * D_EXPERT] * comb[:, e_lane:e_lane + 1])
        hid = jnp.concatenate(parts, axis=1).astype(BF16)
        y = y + _dot(hid, wd_ref[c * cw:(c + 1) * cw, :])
    y_ref[...] = _rms(h + y, gfin_ref[...])


def _moe(h2d, g_ffn, w_rg, b_rg, w_re, b_re, w_gate, w_up, w_down, g_final, tm):
    n = h2d.shape[0]
    assert n % tm == 0
    w_r = jnp.pad(jnp.concatenate([w_rg, w_re], axis=1), ((0, 0), (0, LANES - N_GROUPS - N_EXPERTS)))
    wr_hi, wr_mid, wr_lo = _split3(w_r)
    b_r = jnp.pad(jnp.concatenate([b_rg, b_re]), (0, LANES - N_GROUPS - N_EXPERTS)).reshape(1, LANES)
    wg = jnp.transpose(w_gate, (1, 0, 2)).reshape(D_MODEL, N_EXPERTS * D_EXPERT).astype(BF16)
    wu = jnp.transpose(w_up, (1, 0, 2)).reshape(D_MODEL, N_EXPERTS * D_EXPERT).astype(BF16)
    wd = w_down.reshape(N_EXPERTS * D_EXPERT, D_MODEL).astype(BF16)
    row = pl.BlockSpec((tm, D_MODEL), lambda i: (i, 0))
    full = lambda a: pl.BlockSpec(a.shape, lambda i: (0, 0), pipeline_mode=pl.Buffered(1))
    return pl.pallas_call(
        _moe_body,
        grid=(n // tm,),
        in_specs=[row, full(g_ffn), full(wr_hi), full(wr_mid), full(wr_lo), full(b_r), full(wg), full(wu), full(wd),
                  full(g_final)],
        out_specs=row,
        out_shape=jax.ShapeDtypeStruct((n, D_MODEL), F32),
        compiler_params=_cparams(("parallel",)),
        name="moe",
    )(h2d, g_ffn, wr_hi, wr_mid, wr_lo, b_r, wg, wu, wd, g_final)


def _layer_prompt(x, l, g_mix, w_in, conv_w, cmp_w, w_nsa_out, w_o, moe_w, g_fin):
    b, t, _ = x.shape
    x2d = x.reshape(b * t, D_MODEL)
    gm = g_mix.reshape(1, D_MODEL)
    a, q, qr, kc_r, vc_r, ks, vs, kw, vw, ng, kvb = _proj_in(x2d, gm, w_in, jnp.arange(t), tm=512)
    y_conv = _conv(a.reshape(b, t, C_CONV), None, *conv_w, bb=1, tt=256)
    kc, vc = _compress_prompt(kvb, b, t, cmp_w)
    o = _nsa_prompt(q, qr, ng, kc, vc, kvb, b, t, qb=128, kc_len=512)
    h = _merge(x2d, gm, y_conv.reshape(b * t, D_MODEL), o.reshape(b * t, Q_W), w_in, w_nsa_out, w_o, tm=256)
    y = _moe(h, *moe_w, g_fin, tm=256)
    wb = min(WINDOW, t)
    heads = lambda z: z.reshape(b, t, N_KV, HEAD_DIM)
    state = (heads(kc_r), heads(vc_r), heads(ks), heads(vs), heads(kw)[:, t - wb:], heads(vw)[:, t - wb:],
             a.reshape(b, t, C_CONV)[:, t - (CONV_K - 1):])
    return y.reshape(b, t, D_MODEL), state


def _layer_sample(x, ck_cmp, cv_cmp, ck_sel, cv_sel, sk_win, sv_win, s_conv, page_table,
                  g_mix, w_in, conv_w, cmp_w, w_nsa_out, w_o, moe_w, g_fin):
    b, t, _ = x.shape
    past = page_table.shape[1] * PAGE_SIZE
    x2d = x.reshape(b * t, D_MODEL)
    gm = g_mix.reshape(1, D_MODEL)
    a, q, qr, kc_r, vc_r, ks, vs, kw, vw, ng, kvb = _proj_in(x2d, gm, w_in, past + jnp.arange(t), tm=256)
    a3 = a.reshape(b, t, C_CONV)
    hist = jnp.pad(s_conv, ((0, 0), (_HALO - (CONV_K - 1), 0), (0, 0)))
    y_conv = _conv(a3, hist, *conv_w, bb=16, tt=t)
    kc, vc = _compress_sample(ck_cmp, cv_cmp, page_table, cmp_w)
    o = _nsa_sample(q, qr, ng, kc, vc, kvb, sk_win, sv_win, ck_sel, cv_sel, page_table, tq=t)
    h = _merge(x2d, gm, y_conv.reshape(b * t, D_MODEL), o.reshape(b * t, Q_W), w_in, w_nsa_out, w_o, tm=256)
    y = _moe(h, *moe_w, g_fin, tm=256)
    heads = lambda z: z.reshape(b, t, N_KV, HEAD_DIM)
    k_win = jnp.concatenate([sk_win, heads(kw)], axis=1)[:, t:]
    v_win = jnp.concatenate([sv_win, heads(vw)], axis=1)[:, t:]
    conv_state = jnp.concatenate([s_conv, a3], axis=1)[:, -(CONV_K - 1):]
    state = (heads(kc_r), heads(vc_r), heads(ks), heads(vs), k_win, v_win, conv_state)
    return y.reshape(b, t, D_MODEL), state


def kernel(x_prompt, x_sample, cache_k_cmp, cache_v_cmp, cache_k_sel, cache_v_sel, state_k_win, state_v_win,
           state_conv, page_table, g_mix, w_in, w_dw, b_dw, conv_norm_g, conv_norm_b, w_conv_out,
           w_ck1, b_ck1, w_ck2, w_cv1, b_cv1, w_cv2, w_nsa_out, w_o, g_ffn, w_rg, b_rg, w_re, b_re,
           w_gate, w_up, w_down, g_final):
    depth = g_mix.shape[0]
    assert depth == 1, "single-layer trunk"
    l = 0
    conv_w = (w_dw[l], b_dw[l], conv_norm_g[l], conv_norm_b[l], w_conv_out[l])
    cmp_w = (w_ck1[l], b_ck1[l], w_ck2[l], w_cv1[l], b_cv1[l], w_cv2[l])
    moe_w = (g_ffn[l].reshape(1, D_MODEL), w_rg[l], b_rg[l], w_re[l], b_re[l], w_gate[l], w_up[l], w_down[l])
    g_fin = g_final.reshape(1, D_MODEL)
    y_p, st_p = _layer_prompt(x_prompt, l, g_mix[l], w_in[l], conv_w, cmp_w, w_nsa_out[l], w_o[l], moe_w, g_fin)
    y_s, st_s = _layer_sample(x_sample, cache_k_cmp[l], cache_v_cmp[l], cache_k_sel[l], cache_v_sel[l],
                              state_k_win[l], state_v_win[l], state_conv[l], page_table,
                              g_mix[l], w_in[l], conv_w, cmp_w, w_nsa_out[l], w_o[l], moe_w, g_fin)
    stack = lambda z: z[None]
    return (y_p, y_s) + tuple(stack(z) for z in st_p) + tuple(stack(z) for z in st_s)
```

```python
import functools

import numpy as np
import jax
import jax.numpy as jnp
from jax import lax
from jax.experimental import pallas as pl
from jax.experimental.pallas import tpu as pltpu

D_MODEL = 1024
C_CONV = 512
CONV_K = 31
N_HEADS = 8
N_KV = 2
HEAD_DIM = 64
HPG = N_HEADS // N_KV
CMP_BLOCK = 32
CMP_STRIDE = 16
SEL_BLOCK = 64
N_SEL = 16
WINDOW = 512
PAGE_SIZE = 128
ROPE_THETA = 10000.0
FORCE_BONUS = 1000.0
NEG_INF = -1e30
N_GROUPS = 4
EXP_PER_GROUP = 8
N_EXPERTS = N_GROUPS * EXP_PER_GROUP
D_EXPERT = 128
EPS = 1e-6
KV_W = N_KV * HEAD_DIM
Q_W = N_HEADS * HEAD_DIM
SCALE = HEAD_DIM ** -0.5
HALF = HEAD_DIM // 2

LANES = 128
SUBLANES = 8
VMEM_LIMIT = 56 * 1024 * 1024

BF16 = jnp.bfloat16
F32 = jnp.float32

_OFF_U = 0
_OFF_Q = 2 * C_CONV
_OFF_KV = _OFF_Q + Q_W
_OFF_NG = _OFF_KV + 6 * KV_W
_OFF_MG = _OFF_NG + 3 * N_HEADS
_IN_COLS = _OFF_MG + 2 * D_MODEL
_NG_PAD = 32


def _cparams(sem):
    return pltpu.CompilerParams(dimension_semantics=sem, vmem_limit_bytes=VMEM_LIMIT)


def _dot(a, b):
    return jnp.dot(a, b, preferred_element_type=F32)


def _dot_nt(a, b):
    return lax.dot_general(a, b, (((1,), (1,)), ((), ())), preferred_element_type=F32)


def _rms(x, g):
    ms = jnp.mean(x * x, axis=-1, keepdims=True)
    return x * lax.rsqrt(ms + EPS) * g


def _split3(x):
    hi = x.astype(BF16)
    r1 = x - hi.astype(F32)
    mid = r1.astype(BF16)
    lo = (r1 - mid.astype(F32)).astype(BF16)
    return hi, mid, lo


def _rope_tables(pos):
    inv_freq = ROPE_THETA ** (-jnp.arange(HALF, dtype=F32) * 2.0 / HEAD_DIM)
    ang = pos.astype(F32)[:, None] * inv_freq[None, :]
    return jnp.cos(ang), jnp.sin(ang)


def _rope128(x, cos, sin_signed, first_half):
    swapped = jnp.where(first_half, pltpu.roll(x, LANES - HALF, 1), pltpu.roll(x, HALF, 1))
    return x * cos + swapped * sin_signed


def _rope_t(xh, cos, sin):
    x1, x2 = xh[:HALF], xh[HALF:]
    return jnp.concatenate([x1 * cos - x2 * sin, x2 * cos + x1 * sin], axis=0)


def _proj_in_body(x_ref, g_ref, wu_ref, wq_ref, wkv_ref, wng_ref, cos_ref, sin_ref,
                  a_ref, q_ref, qr_ref, kc_ref, vc_ref, ks_ref, vs_ref, kw_ref, vw_ref, ng_ref, kvb_ref):
    xn = _rms(x_ref[...], g_ref[...]).astype(BF16)
    cos = cos_ref[...]
    sin = sin_ref[...]
    lane = lax.broadcasted_iota(jnp.int32, cos.shape, 1)
    first_half = (lane % HEAD_DIM) < HALF

    u = _dot(xn, wu_ref[...])
    a_ref[...] = u[:, :C_CONV] * jax.nn.sigmoid(u[:, C_CONV:])

    q = _dot(xn, wq_ref[...])
    q_ref[...] = (q * SCALE).astype(BF16)
    for c in range(Q_W // LANES):
        qc = q[:, c * LANES:(c + 1) * LANES]
        qr_ref[:, c * LANES:(c + 1) * LANES] = (_rope128(qc, cos, sin, first_half) * SCALE).astype(BF16)

    kv = _dot(xn, wkv_ref[...])
    kc = kv[:, 0 * KV_W:1 * KV_W]
    vc = kv[:, 1 * KV_W:2 * KV_W]
    ks = _rope128(kv[:, 2 * KV_W:3 * KV_W], cos, sin, first_half)
    vs = kv[:, 3 * KV_W:4 * KV_W]
    kw = _rope128(kv[:, 4 * KV_W:5 * KV_W], cos, sin, first_half)
    vw = kv[:, 5 * KV_W:6 * KV_W]
    kc_ref[...] = kc
    vc_ref[...] = vc
    ks_ref[...] = ks
    vs_ref[...] = vs
    kw_ref[...] = kw
    vw_ref[...] = vw
    for i, piece in enumerate((kc, vc, ks, vs, kw, vw)):
        kvb_ref[:, i * KV_W:(i + 1) * KV_W] = piece.astype(BF16)

    ng_ref[...] = jax.nn.sigmoid(_dot(xn, wng_ref[...]))


def _proj_in(x2d, g_mix, w_in, pos_period, tm):
    n = x2d.shape[0]
    assert n % tm == 0
    p = pos_period.shape[0]
    if p < tm:
        assert tm % p == 0
        pos_period = jnp.tile(pos_period, tm // p)
        p = tm
    assert p % tm == 0
    cos, sin = _rope_tables(pos_period)
    cos = jnp.tile(cos, (1, LANES // HALF))
    sin = jnp.tile(jnp.concatenate([-sin, sin], axis=1), (1, LANES // HEAD_DIM))

    wb = w_in.astype(BF16)
    wu = wb[:, _OFF_U:_OFF_Q]
    wq = wb[:, _OFF_Q:_OFF_KV]
    wkv = wb[:, _OFF_KV:_OFF_NG]
    wng = jnp.pad(wb[:, _OFF_NG:_OFF_MG], ((0, 0), (0, LANES - 3 * N_HEADS)))

    nper = p // tm
    row = lambda w: pl.BlockSpec((tm, w), lambda i: (i, 0))
    full = lambda a: pl.BlockSpec(a.shape, lambda i: (0, 0))
    tab = pl.BlockSpec((tm, LANES), lambda i: (i % nper, 0))
    outs = [
        jax.ShapeDtypeStruct((n, C_CONV), F32),
        jax.ShapeDtypeStruct((n, Q_W), BF16),
        jax.ShapeDtypeStruct((n, Q_W), BF16),
    ] + [jax.ShapeDtypeStruct((n, KV_W), F32)] * 6 + [
        jax.ShapeDtypeStruct((n, LANES), F32),
        jax.ShapeDtypeStruct((n, 6 * KV_W), BF16),
    ]
    out_specs = [row(C_CONV), row(Q_W), row(Q_W)] + [row(KV_W)] * 6 + [row(LANES), row(6 * KV_W)]
    return pl.pallas_call(
        _proj_in_body,
        grid=(n // tm,),
        in_specs=[row(D_MODEL), full(g_mix), full(wu), full(wq), full(wkv), full(wng), tab, tab],
        out_specs=out_specs,
        out_shape=outs,
        compiler_params=_cparams(("parallel",)),
        name="proj_in",
    )(x2d, g_mix, wu, wq, wkv, wng, cos, sin)


_ZT_ROWS = Q_W + 6 * KV_W + _NG_PAD


def _proj_in_t_body(x_ref, g_ref, wu_ref, wt_ref, cos_ref, sin_ref,
                    a_ref, qt_ref, qrt_ref, kct_ref, vct_ref, kst_ref, vst_ref, kwt_ref, vwt_ref, ngt_ref,
                    kc_ref, vc_ref, ks_ref, kw_ref, vstb_ref, vwtb_ref, *, tm):
    xn = _rms(x_ref[...], g_ref[...]).astype(BF16)
    cos = cos_ref[...]
    sin = sin_ref[...]

    u = _dot(xn, wu_ref[...])
    a_ref[...] = u[:, :C_CONV] * jax.nn.sigmoid(u[:, C_CONV:])

    z = _dot_nt(wt_ref[...], xn)
    for h in range(N_HEADS):
        qh = z[h * HEAD_DIM:(h + 1) * HEAD_DIM]
        qt_ref[0, h * HEAD_DIM:(h + 1) * HEAD_DIM, :] = (qh * SCALE).astype(BF16)
        qrt_ref[0, h * HEAD_DIM:(h + 1) * HEAD_DIM, :] = (_rope_t(qh, cos, sin) * SCALE).astype(BF16)

    def kv_rows(i):
        return z[Q_W + i * KV_W:Q_W + (i + 1) * KV_W]

    def rope_groups(x):
        return jnp.concatenate([_rope_t(x[g * HEAD_DIM:(g + 1) * HEAD_DIM], cos, sin) for g in range(N_KV)], axis=0)

    kct, vct, vst, vwt = kv_rows(0), kv_rows(1), kv_rows(3), kv_rows(5)
    kst = rope_groups(kv_rows(2))
    kwt = rope_groups(kv_rows(4))
    kct_ref[0] = kct
    vct_ref[0] = vct
    kst_ref[0] = kst
    vst_ref[0] = vst
    kwt_ref[0] = kwt
    vwt_ref[0] = vwt
    kc_ref[...] = kct.T.astype(BF16)
    vc_ref[...] = vct.T.astype(BF16)
    ks_ref[...] = kst.T.astype(BF16)
    kw_ref[...] = kwt.T.astype(BF16)
    for j in range(tm // LANES):
        vstb_ref[0, j] = vst[:, j * LANES:(j + 1) * LANES].astype(BF16)
        vwtb_ref[0, j] = vwt[:, j * LANES:(j + 1) * LANES].astype(BF16)

    ngt_ref[0] = jax.nn.sigmoid(z[Q_W + 6 * KV_W:])


def _proj_in_t(x, g_mix, w_in, tm):
    b, t, _ = x.shape
    assert t % tm == 0 and tm % LANES == 0
    n = b * t
    nt = t // tm
    cos, sin = _rope_tables(jnp.arange(t))
    cos_t, sin_t = cos.T, sin.T
    wu = w_in[:, _OFF_U:_OFF_Q].astype(BF16)
    wt = jnp.pad(w_in.T[_OFF_Q:_OFF_MG], ((0, _NG_PAD - 3 * N_HEADS), (0, 0))).astype(BF16)

    row = lambda w: pl.BlockSpec((tm, w), lambda i: (i, 0))
    full = lambda a: pl.BlockSpec(a.shape, lambda i: (0, 0))
    tab = pl.BlockSpec((HALF, tm), lambda i: (0, i % nt))
    feat = lambda r: pl.BlockSpec((1, r, tm), lambda i: (i // nt, 0, i % nt))
    chunks = pl.BlockSpec((1, tm // LANES, KV_W, LANES), lambda i: (i // nt, i % nt, 0, 0))
    outs = (
        [jax.ShapeDtypeStruct((n, C_CONV), F32),
         jax.ShapeDtypeStruct((b, Q_W, t), BF16), jax.ShapeDtypeStruct((b, Q_W, t), BF16)]
        + [jax.ShapeDtypeStruct((b, KV_W, t), F32)] * 6
        + [jax.ShapeDtypeStruct((b, _NG_PAD, t), F32)]
        + [jax.ShapeDtypeStruct((n, KV_W), BF16)] * 4
        + [jax.ShapeDtypeStruct((b, t // LANES, KV_W, LANES), BF16)] * 2
    )
    out_specs = [row(C_CONV), feat(Q_W), feat(Q_W)] + [feat(KV_W)] * 6 + [feat(_NG_PAD)] + [row(KV_W)] * 4 + [chunks] * 2
    return pl.pallas_call(
        functools.partial(_proj_in_t_body, tm=tm),
        grid=(n // tm,),
        in_specs=[row(D_MODEL), full(g_mix), full(wu), full(wt), tab, tab],
        out_specs=out_specs,
        out_shape=outs,
        compiler_params=_cparams(("parallel",)),
        name="proj_in_t",
    )(x.reshape(n, D_MODEL), g_mix, wu, wt, cos_t, sin_t)


_HALO = 32


def _conv_body(a_ref, halo_ref, wdw_ref, bdw_ref, ng_ref, nb_ref, wout_ref, y_ref, full_ref, c_ref,
               *, bb, tt, halo_from_prev_tile):
    if halo_from_prev_tile:
        first = pl.program_id(1) == 0

        @pl.when(first)
        def _():
            full_ref[:, 0:_HALO, :] = jnp.zeros((bb, _HALO, C_CONV), F32)

        @pl.when(jnp.logical_not(first))
        def _():
            full_ref[:, 0:_HALO, :] = halo_ref[...]
    else:
        full_ref[:, 0:_HALO, :] = halo_ref[...]
    full_ref[:, _HALO:_HALO + tt, :] = a_ref[...]

    base = _HALO - (CONV_K - 1)
    rc = min(tt, 64)
    for s in range(bb):
        for r0 in range(0, tt, rc):
            for l0 in range(0, C_CONV, LANES):
                acc = jnp.zeros((rc, LANES), F32)
                for k in range(CONV_K):
                    rows = slice(base + r0 + k, base + r0 + k + rc)
                    acc = acc + full_ref[s, rows, l0:l0 + LANES] * wdw_ref[k:k + 1, l0:l0 + LANES]
                c_ref[s * tt + r0:s * tt + r0 + rc, l0:l0 + LANES] = acc + bdw_ref[:, l0:l0 + LANES]

    c = c_ref[...]
    mu = jnp.mean(c, axis=-1, keepdims=True)
    d = c - mu
    var = jnp.mean(d * d, axis=-1, keepdims=True)
    cn = d * lax.rsqrt(var + EPS) * ng_ref[...] + nb_ref[...]
    act = (cn * jax.nn.sigmoid(cn)).astype(BF16)
    y_ref[...] = _dot(act, wout_ref[...]).reshape(bb, tt, D_MODEL)


def _conv(a3, hist, w_dw, b_dw, n_g, n_b, w_out, bb, tt):
    b, t, _ = a3.shape
    assert t % tt == 0 and b % bb == 0 and tt % SUBLANES == 0
    wdw = jnp.pad(w_dw, ((0, 1), (0, 0)))
    wout = w_out.astype(BF16)
    from_prev = hist is None
    if from_prev:
        assert tt % _HALO == 0
        ratio = tt // _HALO
        halo_arr = a3
        halo_spec = pl.BlockSpec((bb, _HALO, C_CONV), lambda bi, i: (bi, jnp.maximum(i * ratio - 1, 0), 0))
    else:
        assert t == tt
        halo_arr = hist
        halo_spec = pl.BlockSpec((bb, _HALO, C_CONV), lambda bi, i: (bi, 0, 0))
    vec = pl.BlockSpec((1, C_CONV), lambda bi, i: (0, 0))
    return pl.pallas_call(
        functools.partial(_conv_body, bb=bb, tt=tt, halo_from_prev_tile=from_prev),
        grid=(b // bb, t // tt),
        in_specs=[
            pl.BlockSpec((bb, tt, C_CONV), lambda bi, i: (bi, i, 0)),
            halo_spec,
            pl.BlockSpec(wdw.shape, lambda bi, i: (0, 0)),
            vec, vec, vec,
            pl.BlockSpec(wout.shape, lambda bi, i: (0, 0)),
        ],
        out_specs=pl.BlockSpec((bb, tt, D_MODEL), lambda bi, i: (bi, i, 0)),
        out_shape=jax.ShapeDtypeStruct((b, t, D_MODEL), F32),
        scratch_shapes=[pltpu.VMEM((bb, _HALO + tt, C_CONV), F32), pltpu.VMEM((bb * tt, C_CONV), F32)],
        compiler_params=_cparams(("parallel", "arbitrary")),
        name="conv",
    )(a3, halo_arr, wdw, b_dw.reshape(1, -1), n_g.reshape(1, -1), n_b.reshape(1, -1), wout)


def _compress_weights(w1, b1, w2):
    eye = jnp.eye(N_KV, dtype=F32)
    big = jnp.einsum("gh,lde->lgdhe", eye, w1).reshape(CMP_BLOCK, KV_W, KV_W)
    wa = big[:CMP_STRIDE].reshape(CMP_STRIDE * KV_W, KV_W)
    wb = big[CMP_STRIDE:].reshape(CMP_STRIDE * KV_W, KV_W)
    wcat = jnp.concatenate([wa, wb], axis=1).astype(BF16)
    w2bd = jnp.einsum("gh,ef->gehf", eye, w2).reshape(KV_W, KV_W)
    b1t = jnp.tile(b1, N_KV).reshape(1, KV_W)
    return wcat, b1t, w2bd.astype(BF16), w2bd.T.astype(BF16)


def _compress_hidden(p, b1_ref):
    s = p.shape[0]
    pre = p[:, :KV_W] + pltpu.roll(p[:, KV_W:], s - 1, 0) + b1_ref[...]
    return (pre * jax.nn.sigmoid(pre)).astype(BF16)


def _compress_prompt_body(k_ref, v_ref, wk_ref, bk_ref, wk2_ref, wv_ref, bv_ref, wv2t_ref, kc_ref, vct_ref):
    hid_k = _compress_hidden(_dot(k_ref[0], wk_ref[...]), bk_ref)
    kc_ref[0] = _dot(hid_k, wk2_ref[...]).astype(BF16)
    hid_v = _compress_hidden(_dot(v_ref[0], wv_ref[...]), bv_ref)
    vct_ref[0] = _dot_nt(wv2t_ref[...], hid_v).astype(BF16)


def _compress_prompt(kc_rm, vc_rm, b, t, cmp_w):
    s = t // CMP_STRIDE
    feat = CMP_STRIDE * KV_W
    k_rows = kc_rm.reshape(b, s, feat)
    v_rows = vc_rm.reshape(b, s, feat)
    wk, bk, wk2, _ = _compress_weights(*cmp_w[:3])
    wv, bv, _, wv2t = _compress_weights(*cmp_w[3:])
    rows = pl.BlockSpec((1, s, feat), lambda i: (i, 0, 0))
    full = lambda a: pl.BlockSpec(a.shape, lambda i: (0, 0))
    return pl.pallas_call(
        _compress_prompt_body,
        grid=(b,),
        in_specs=[rows, rows, full(wk), full(bk), full(wk2), full(wv), full(bv), full(wv2t)],
        out_specs=[pl.BlockSpec((1, s, KV_W), lambda i: (i, 0, 0)), pl.BlockSpec((1, KV_W, s), lambda i: (i, 0, 0))],
        out_shape=[jax.ShapeDtypeStruct((b, s, KV_W), BF16), jax.ShapeDtypeStruct((b, KV_W, s), BF16)],
        compiler_params=_cparams(("parallel",)),
        name="compress_prompt",
    )(k_rows, v_rows, wk, bk, wk2, wv, bv, wv2t)


def _gather_pages(pools, bufs, sems, pt_ref, seq, slot, n_pages, dst_of, wait):
    def body(j, carry):
        page = pt_ref[seq * n_pages + j]
        for pool_ref, buf_ref, sem_ref in zip(pools, bufs, sems):
            cp = pltpu.make_async_copy(pool_ref.at[page], dst_of(buf_ref, slot, j), sem_ref.at[slot])
            if wait:
                cp.wait()
            else:
                cp.start()
        return carry
    lax.fori_loop(0, n_pages, body, 0)


def _paged_prologue(pools, bufs, sems, pt_ref, n_pages, dst_of):
    i = pl.program_id(0)
    n = pl.num_programs(0)
    slot = i % 2

    @pl.when(i == 0)
    def _():
        _gather_pages(pools, bufs, sems, pt_ref, i, slot, n_pages, dst_of, wait=False)

    @pl.when(i + 1 < n)
    def _():
        _gather_pages(pools, bufs, sems, pt_ref, i + 1, 1 - slot, n_pages, dst_of, wait=False)

    _gather_pages(pools, bufs, sems, pt_ref, i, slot, n_pages, dst_of, wait=True)
    return slot


def _dst_page_major(buf_ref, slot, j):
    return buf_ref.at[slot, j]


def _dst_token_lanes(buf_ref, slot, j):
    return buf_ref.at[slot, :, pl.ds(pl.multiple_of(j * PAGE_SIZE, PAGE_SIZE), PAGE_SIZE)]


def _compress_sample_body(pt_ref, kpool_ref, vpool_ref, wk_ref, bk_ref, wk2_ref, wv_ref, bv_ref, wv2_ref,
                          kc_ref, vc_ref, kbuf, vbuf, ksem, vsem, tok_ref, *, n_pages):
    slot = _paged_prologue((kpool_ref, vpool_ref), (kbuf, vbuf), (ksem, vsem), pt_ref, n_pages, _dst_page_major)
    n_rows = n_pages * PAGE_SIZE // CMP_STRIDE

    def compress(buf, w_ref, b_ref, w2_ref, out_ref):
        def tr(j, carry):
            tok_ref[pl.ds(pl.multiple_of(j * PAGE_SIZE, PAGE_SIZE), PAGE_SIZE), :] = buf[slot, j].T
            return carry
        lax.fori_loop(0, n_pages, tr, 0)
        p = jnp.zeros((n_rows, 2 * KV_W), F32)
        for l in range(CMP_STRIDE):
            y_l = tok_ref[pl.ds(l, n_rows, stride=CMP_STRIDE), :].astype(BF16)
            p = p + _dot(y_l, w_ref[l * KV_W:(l + 1) * KV_W, :])
        out_ref[0] = _dot(_compress_hidden(p, b_ref), w2_ref[...]).astype(BF16)

    compress(kbuf, wk_ref, bk_ref, wk2_ref, kc_ref)
    compress(vbuf, wv_ref, bv_ref, wv2_ref, vc_ref)


def _compress_sample(pool_k_t, pool_v_t, page_table, cmp_w):
    b, n_pages = page_table.shape
    s = n_pages * PAGE_SIZE // CMP_STRIDE
    wk, bk, wk2, _ = _compress_weights(*cmp_w[:3])
    wv, bv, wv2, _ = _compress_weights(*cmp_w[3:])
    full = lambda a: pl.BlockSpec(a.shape, lambda i, pt: (0, 0))
    out = pl.BlockSpec((1, s, KV_W), lambda i, pt: (i, 0, 0))
    anyspec = pl.BlockSpec(memory_space=pl.ANY)
    grid_spec = pltpu.PrefetchScalarGridSpec(
        num_scalar_prefetch=1,
        grid=(b,),
        in_specs=[anyspec, anyspec, full(wk), full(bk), full(wk2), full(wv), full(bv), full(wv2)],
        out_specs=[out, out],
        scratch_shapes=[
            pltpu.VMEM((2, n_pages, KV_W, PAGE_SIZE), F32),
            pltpu.VMEM((2, n_pages, KV_W, PAGE_SIZE), F32),
            pltpu.SemaphoreType.DMA((2,)),
            pltpu.SemaphoreType.DMA((2,)),
            pltpu.VMEM((n_pages * PAGE_SIZE, KV_W), F32),
        ],
    )
    return pl.pallas_call(
        functools.partial(_compress_sample_body, n_pages=n_pages),
        grid_spec=grid_spec,
        out_shape=[jax.ShapeDtypeStruct((b, s, KV_W), BF16)] * 2,
        compiler_params=_cparams(("arbitrary",)),
        name="compress_sample",
    )(page_table.reshape(-1), pool_k_t, pool_v_t, wk, bk, wk2, wv, bv, wv2)


def _overlap_np(nc_pad, ns_pad):
    i = np.arange(nc_pad)[:, None]
    j = np.arange(ns_pad)[None, :]
    return ((i * CMP_STRIDE < (j + 1) * SEL_BLOCK) & (i * CMP_STRIDE + CMP_BLOCK > j * SEL_BLOCK)).astype(np.float32)


def _importance_bonus(imp, j, t):
    cur = t // SEL_BLOCK
    forced = (j == 0) | (j == cur) | (j == cur - 1)
    blk_ok = j * SEL_BLOCK <= t
    return jnp.where(blk_ok, imp + jnp.where(forced, FORCE_BONUS, 0.0), NEG_INF)


def _softmax_cols(s):
    m = jnp.max(s, axis=0, keepdims=True)
    e = jnp.exp(s - m)
    return e / jnp.sum(e, axis=0, keepdims=True)


def _top_k_cols(imp, k):
    ns = imp.shape[0]
    j = lax.broadcasted_iota(jnp.int32, imp.shape, 0)
    cnt = jnp.zeros(imp.shape, F32)
    for i in range(ns):
        row = imp[i:i + 1, :]
        tie = jnp.where(j > i, 1.0, 0.0)
        cnt = cnt + jnp.where(row > imp, 1.0, jnp.where(row == imp, tie, 0.0))
    return jnp.where(cnt < k, 1.0, 0.0)


def _heads_on_lanes(xt, g, rows_per_head):
    return jnp.concatenate(
        [xt[(g * HPG + h) * rows_per_head:(g * HPG + h + 1) * rows_per_head] for h in range(HPG)], axis=1)


def _gate_row(ngt, g, branch):
    return jnp.concatenate([ngt[(g * HPG + h) * 3 + branch:(g * HPG + h) * 3 + branch + 1] for h in range(HPG)], axis=1)


def _nsa_prompt_body(qt_ref, qrt_ref, ngt_ref, kc_ref, vct_ref, ks_ref, kw_ref, vst_ref, vwt_ref, ovt_ref, et_ref,
                     o_ref, *, qb, kc_len):
    i = pl.program_id(1)
    start = i * qb
    t_row = start + lax.broadcasted_iota(jnp.int32, (1, qb), 1)
    t4 = jnp.concatenate([t_row] * HPG, axis=1)
    qt = qt_ref[0]
    qrt = qrt_ref[0]
    ngt = ngt_ref[0]
    nc = kc_ref.shape[1]
    ns = ovt_ref.shape[0]
    n_chunks = (start + qb + kc_len - 1) // kc_len
    sub = kc_len // LANES
    w_len = WINDOW + qb
    w_start = pl.multiple_of(jnp.maximum(start - WINDOW, 0), qb)
    w_blk = w_start // LANES

    o_heads = []
    for g in range(N_KV):
        glanes = slice(g * HEAD_DIM, (g + 1) * HEAD_DIM)
        q_g = _heads_on_lanes(qt, g, HEAD_DIM)
        qr_g = _heads_on_lanes(qrt, g, HEAD_DIM)

        s_c = _dot(kc_ref[0][:, glanes], q_g)
        c_end = lax.broadcasted_iota(jnp.int32, (nc, 1), 0) * CMP_STRIDE + (CMP_BLOCK - 1)
        p_c = _softmax_cols(jnp.where(c_end <= t4, s_c, NEG_INF))
        p_c = p_c * jnp.where(t4 >= CMP_BLOCK - 1, 1.0, 0.0)
        o_c = _dot(vct_ref[0][glanes, :], p_c.astype(BF16))
        p_sum = p_c[:, 0:qb] + p_c[:, qb:2 * qb] + p_c[:, 2 * qb:3 * qb] + p_c[:, 3 * qb:4 * qb]
        hi, mid, lo = _split3(p_sum)
        ovt = ovt_ref[...]
        imp = _dot(ovt, hi) + _dot(ovt, mid) + _dot(ovt, lo)
        j_blk = lax.broadcasted_iota(jnp.int32, (ns, qb), 0)
        sel = _top_k_cols(_importance_bonus(imp, j_blk, t_row), min(N_SEL, ns)).astype(BF16)

        def chunk(c, carry):
            m, l, acc = carry
            k0 = pl.multiple_of(c * kc_len, kc_len)
            kch = ks_ref[0, pl.ds(k0, kc_len), glanes]
            vblk = vst_ref[0, pl.ds(c * sub, sub)]
            vch = jnp.concatenate([vblk[u][glanes, :] for u in range(sub)], axis=1)
            key = k0 + lax.broadcasted_iota(jnp.int32, (kc_len, 1), 0)
            ok = (_dot(et_ref[c], sel) > 0.5) & (key <= t_row)
            bias = jnp.where(ok, 0.0, NEG_INF)
            s = _dot(kch, qr_g) + jnp.concatenate([bias] * HPG, axis=1)
            m_new = jnp.maximum(m, jnp.max(s, axis=0, keepdims=True))
            alpha = jnp.exp(m - m_new)
            p = jnp.exp(s - m_new)
            l = alpha * l + jnp.sum(p, axis=0, keepdims=True)
            acc = alpha * acc + _dot(vch, p.astype(BF16))
            return m_new, l, acc

        init = (jnp.full((1, HPG * qb), NEG_INF, F32), jnp.zeros((1, HPG * qb), F32), jnp.zeros((HEAD_DIM, HPG * qb), F32))
        _, l, acc = lax.fori_loop(0, n_chunks, chunk, init)
        o_s = acc / l

        kwc = kw_ref[0, pl.ds(w_start, w_len), glanes]
        wblk = vwt_ref[0, pl.ds(w_blk, w_len // LANES)]
        vwc = jnp.concatenate([wblk[u][glanes, :] for u in range(w_len // LANES)], axis=1)
        key = w_start + lax.broadcasted_iota(jnp.int32, (w_len, 1), 0)
        ok = (key <= t4) & (t4 - key < WINDOW)
        p_w = _softmax_cols(jnp.where(ok, _dot(kwc, qr_g), NEG_INF))
        o_w = _dot(vwc, p_w.astype(BF16))

        o = _gate_row(ngt, g, 0) * o_c + _gate_row(ngt, g, 1) * o_s + _gate_row(ngt, g, 2) * o_w
        o_heads += [o[:, h * qb:(h + 1) * qb] for h in range(HPG)]
    o_ref[0] = jnp.concatenate(o_heads, axis=0).T.astype(o_ref.dtype)


def _nsa_prompt(qt, qrt, ngt, kc, vct, ks_rm, kw_rm, vstb, vwtb, b, t, qb, kc_len):
    assert t % kc_len == 0 and kc_len % qb == 0 and t >= WINDOW + qb and WINDOW % qb == 0 and qb % LANES == 0
    ns = t // SEL_BLOCK
    nc_pad = t // CMP_STRIDE
    ovt = jnp.asarray(_overlap_np(nc_pad, ns).T, dtype=BF16)
    blk = np.arange(t) // SEL_BLOCK
    et = (blk[:, None] == np.arange(ns)[None, :]).astype(np.float32)
    et3 = jnp.asarray(et.reshape(t // kc_len, kc_len, ns), dtype=BF16)
    tile = lambda r: pl.BlockSpec((1, r, qb), lambda bi, i: (bi, 0, i))
    per_b = lambda shp: pl.BlockSpec((1,) + shp, lambda bi, i: (bi,) + (0,) * len(shp))
    const = lambda a: pl.BlockSpec(a.shape, lambda bi, i: (0,) * a.ndim)
    return pl.pallas_call(
        functools.partial(_nsa_prompt_body, qb=qb, kc_len=kc_len),
        grid=(b, t // qb),
        in_specs=[
            tile(Q_W), tile(Q_W), tile(_NG_PAD),
            per_b((nc_pad, KV_W)), per_b((KV_W, nc_pad)),
            per_b((t, KV_W)), per_b((t, KV_W)),
            per_b((t // LANES, KV_W, LANES)), per_b((t // LANES, KV_W, LANES)),
            const(ovt), const(et3),
        ],
        out_specs=pl.BlockSpec((1, qb, Q_W), lambda bi, i: (bi, i, 0)),
        out_shape=jax.ShapeDtypeStruct((b, t, Q_W), BF16),
        compiler_params=_cparams(("parallel", "arbitrary")),
        name="nsa_prompt",
    )(qt, qrt, ngt, kc, vct, ks_rm.reshape(b, t, KV_W), kw_rm.reshape(b, t, KV_W), vstb, vwtb, ovt, et3)


def _block_diag_q(q, tq):
    z = jnp.zeros((HPG * tq, HEAD_DIM), q.dtype)
    rows = []
    for g in range(N_KV):
        qs = jnp.concatenate([q[:, (g * HPG + h) * HEAD_DIM:(g * HPG + h + 1) * HEAD_DIM] for h in range(HPG)], axis=0)
        rows.append(jnp.concatenate([qs, z] if g == 0 else [z, qs], axis=1))
    return jnp.concatenate(rows, axis=0)


def _rows_ghq(x, tq):
    return jnp.concatenate([x[g * tq:(g + 1) * tq] for g in range(N_KV) for _ in range(HPG)], axis=0)


def _top_k_rows(imp, imp_t, k):
    r, ns = imp.shape
    ii = lax.broadcasted_iota(jnp.int32, (ns, ns), 0)
    jj = lax.broadcasted_iota(jnp.int32, (ns, ns), 1)
    tie = jnp.where(ii < jj, 1.0, 0.0)
    out = []
    for x in range(r):
        col = imp_t[:, x:x + 1]
        row = imp[x:x + 1, :]
        beats = jnp.where(col > row, 1.0, jnp.where(col == row, tie, 0.0))
        out.append(jnp.sum(beats, axis=0, keepdims=True))
    return jnp.where(jnp.concatenate(out, axis=0) < k, 1.0, 0.0)


def _nsa_sample_body(pt_ref, q_ref, qr_ref, ng_ref, kc_ref, vc_ref, kvn_ref, kwin_ref, vwin_ref, kpool_ref, vpool_ref,
                     ov_ref, e_ref, o_ref, kbuf, vbuf, ksem, vsem, kcat_ref, *, n_pages, tq):
    slot = _paged_prologue((kpool_ref, vpool_ref), (kbuf, vbuf), (ksem, vsem), pt_ref, n_pages, _dst_token_lanes)
    past = n_pages * PAGE_SIZE
    ns_past = past // SEL_BLOCK
    nrow = N_KV * HPG * tq
    wb = kwin_ref.shape[2]

    @pl.when(pl.program_id(0) == 0)
    def _():
        kcat_ref[KV_W:, :] = e_ref[...]

    t_q = past + lax.broadcasted_iota(jnp.int32, (tq, 1), 0)
    t16 = jnp.concatenate([t_q] * N_KV, axis=0)
    t_rows = _rows_ghq(t16, tq)
    row_g = lax.broadcasted_iota(jnp.int32, (nrow, 1), 0) // (HPG * tq)

    def own_group(x):
        return jnp.where(row_g == 0, x[:, :HEAD_DIM], x[:, HEAD_DIM:])

    q_bd = _block_diag_q(q_ref[0], tq)
    qr_bd = _block_diag_q(qr_ref[0], tq)
    kvn = kvn_ref[0]
    new = lambda idx: kvn[:, idx * KV_W:(idx + 1) * KV_W]

    nc = kc_ref.shape[1]
    s_c = _dot_nt(q_bd, kc_ref[0])
    c_end = lax.broadcasted_iota(jnp.int32, (1, nc), 1) * CMP_STRIDE + (CMP_BLOCK - 1)
    s_c = jnp.where(c_end <= t_rows, s_c, NEG_INF)
    m_c = jnp.max(s_c, axis=-1, keepdims=True)
    e_c = jnp.exp(s_c - m_c)
    p_c = e_c / jnp.sum(e_c, axis=-1, keepdims=True) * jnp.where(t_rows >= CMP_BLOCK - 1, 1.0, 0.0)
    o_c = own_group(_dot(p_c.astype(BF16), vc_ref[0]))
    p_sum = jnp.concatenate(
        [sum(p_c[(g * HPG + h) * tq:(g * HPG + h + 1) * tq] for h in range(HPG)) for g in range(N_KV)], axis=0)
    hi, mid, lo = _split3(p_sum)
    ov = ov_ref[...]
    imp = _dot(hi, ov) + _dot(mid, ov) + _dot(lo, ov)
    ns_pad = imp.shape[1]
    j_blk = lax.broadcasted_iota(jnp.int32, imp.shape, 1)
    imp = _importance_bonus(imp, j_blk, t16)
    imp_sq = jnp.concatenate([imp, jnp.zeros((ns_pad - N_KV * tq, ns_pad), F32)], axis=0)
    sel = _top_k_rows(imp, imp_sq.T, min(N_SEL, ns_past + 1))

    bias = jnp.where(sel[:, :ns_past] > 0.5, 0.0, NEG_INF)
    lhs = jnp.concatenate([qr_bd, _rows_ghq(bias, tq).astype(BF16)], axis=1)
    kcat_ref[:KV_W, :] = kbuf[slot].astype(BF16)
    s_past = _dot(lhs, kcat_ref[...])
    new_sel = _rows_ghq(sel[:, ns_past:ns_past + 1], tq) > 0.5
    key_new = past + lax.broadcasted_iota(jnp.int32, (1, tq), 1)
    causal_new = key_new <= t_rows
    s_new = jnp.where(new_sel & causal_new, _dot_nt(qr_bd, new(2)), NEG_INF)
    m = jnp.maximum(jnp.max(s_past, axis=-1, keepdims=True), jnp.max(s_new, axis=-1, keepdims=True))
    p_past = jnp.exp(s_past - m)
    p_new = jnp.exp(s_new - m)
    l = jnp.sum(p_past, axis=-1, keepdims=True) + jnp.sum(p_new, axis=-1, keepdims=True)
    o_s = own_group(_dot_nt(p_past.astype(BF16), vbuf[slot].astype(BF16)) + _dot(p_new.astype(BF16), new(3))) / l

    pos_p = (past - wb) + lax.broadcasted_iota(jnp.int32, (1, wb), 1)
    ok_p = (pos_p <= t_rows) & (t_rows - pos_p < WINDOW) & (pos_p >= 0)
    sw_p = jnp.where(ok_p, _dot(qr_bd, kwin_ref[0].astype(BF16)), NEG_INF)
    sw_n = jnp.where(causal_new, _dot_nt(qr_bd, new(4)), NEG_INF)
    mw = jnp.maximum(jnp.max(sw_p, axis=-1, keepdims=True), jnp.max(sw_n, axis=-1, keepdims=True))
    pw_p = jnp.exp(sw_p - mw)
    pw_n = jnp.exp(sw_n - mw)
    lw = jnp.sum(pw_p, axis=-1, keepdims=True) + jnp.sum(pw_n, axis=-1, keepdims=True)
    o_w = own_group(_dot_nt(pw_p.astype(BF16), vwin_ref[0].astype(BF16)) + _dot(pw_n.astype(BF16), new(5))) / lw

    ng = ng_ref[0]
    gate = lambda br: jnp.concatenate([ng[:, hh * 3 + br:hh * 3 + br + 1] for hh in range(N_HEADS)], axis=0)
    o = gate(0) * o_c + gate(1) * o_s + gate(2) * o_w
    for hh in range(N_HEADS):
        o_ref[0, :, hh * HEAD_DIM:(hh + 1) * HEAD_DIM] = o[hh * tq:(hh + 1) * tq].astype(o_ref.dtype)


def _nsa_sample(q, qr, ng, kc, vc, kvb, k_win_t, v_win_t, pool_k_t, pool_v_t, page_table, tq):
    b, n_pages = page_table.shape
    past = n_pages * PAGE_SIZE
    assert tq < CMP_STRIDE and tq <= SEL_BLOCK and past % SEL_BLOCK == 0 and tq % SUBLANES == 0
    wb = k_win_t.shape[2]
    ns_past = past // SEL_BLOCK
    ns_pad = (ns_past + 1 + LANES - 1) // LANES * LANES
    nc_pad = past // CMP_STRIDE
    ov = jnp.asarray(_overlap_np(nc_pad, ns_pad), dtype=BF16)
    blk = np.arange(past) // SEL_BLOCK
    e = jnp.asarray((np.arange(ns_past)[:, None] == blk[None, :]).astype(np.float32), dtype=BF16)
    per_b = lambda rows, w: pl.BlockSpec((1, rows, w), lambda i, pt: (i, 0, 0))
    anyspec = pl.BlockSpec(memory_space=pl.ANY)
    grid_spec = pltpu.PrefetchScalarGridSpec(
        num_scalar_prefetch=1,
        grid=(b,),
        in_specs=[
            per_b(tq, Q_W), per_b(tq, Q_W), per_b(tq, LANES),
            per_b(nc_pad, KV_W), per_b(nc_pad, KV_W), per_b(tq, 6 * KV_W),
            per_b(KV_W, wb), per_b(KV_W, wb),
            anyspec, anyspec,
            pl.BlockSpec(ov.shape, lambda i, pt: (0, 0)),
            pl.BlockSpec(e.shape, lambda i, pt: (0, 0)),
        ],
        out_specs=per_b(tq, Q_W),
        scratch_shapes=[
            pltpu.VMEM((2, KV_W, past), F32),
            pltpu.VMEM((2, KV_W, past), F32),
            pltpu.SemaphoreType.DMA((2,)),
            pltpu.SemaphoreType.DMA((2,)),
            pltpu.VMEM((KV_W + ns_past, past), BF16),
        ],
    )
    return pl.pallas_call(
        functools.partial(_nsa_sample_body, n_pages=n_pages, tq=tq),
        grid_spec=grid_spec,
        out_shape=jax.ShapeDtypeStruct((b, tq, Q_W), BF16),
        compiler_params=_cparams(("arbitrary",)),
        name="nsa_sample",
    )(page_table.reshape(-1), q.reshape(b, tq, Q_W), qr.reshape(b, tq, Q_W), ng.reshape(b, tq, LANES), kc, vc,
      kvb.reshape(b, tq, 6 * KV_W), k_win_t, v_win_t, pool_k_t, pool_v_t, ov, e)


def _merge_body(x_ref, g_ref, yc_ref, o_ref, wmg_ref, wn_ref, wo_ref, h_ref):
    x = x_ref[...]
    xn = _rms(x, g_ref[...]).astype(BF16)
    mg = jax.nn.sigmoid(_dot(xn, wmg_ref[...]))
    y_nsa = _dot(o_ref[...], wn_ref[...])
    mix = mg[:, :D_MODEL] * yc_ref[...] + mg[:, D_MODEL:] * y_nsa
    h_ref[...] = x + _dot(mix.astype(BF16), wo_ref[...])


def _merge(x2d, g_mix, y_conv, o, w_in, w_nsa_out, w_o, tm):
    n = x2d.shape[0]
    assert n % tm == 0
    wmg = w_in[:, _OFF_MG:_IN_COLS].astype(BF16)
    wn = w_nsa_out.astype(BF16)
    wo = w_o.astype(BF16)
    row = lambda w: pl.BlockSpec((tm, w), lambda i: (i, 0))
    full = lambda a: pl.BlockSpec(a.shape, lambda i: (0, 0))
    return pl.pallas_call(
        _merge_body,
        grid=(n // tm,),
        in_specs=[row(D_MODEL), full(g_mix), row(D_MODEL), row(Q_W), full(wmg), full(wn), full(wo)],
        out_specs=row(D_MODEL),
        out_shape=jax.ShapeDtypeStruct((n, D_MODEL), F32),
        compiler_params=_cparams(("parallel",)),
        name="merge",
    )(x2d, g_mix, y_conv, o, wmg, wn, wo)


_EXP_CHUNK = 2


def _moe_body(h_ref, gf_ref, wr_hi_ref, wr_mid_ref, wr_lo_ref, br_ref, wg_ref, wu_ref, wd_ref, gfin_ref, y_ref):
    h = h_ref[...]
    xn = _rms(h, gf_ref[...])
    tm = h.shape[0]

    x_hi, x_mid, x_lo = _split3(xn)
    w_hi, w_mid, w_lo = wr_hi_ref[...], wr_mid_ref[...], wr_lo_ref[...]
    logits = (_dot(x_hi, w_hi) + (_dot(x_hi, w_mid) + _dot(x_mid, w_hi))
              + (_dot(x_hi, w_lo) + _dot(x_mid, w_mid) + _dot(x_lo, w_hi))) + br_ref[...]
    lane = lax.broadcasted_iota(jnp.int32, logits.shape, 1)
    is_grp = lane < N_GROUPS
    lg = jnp.where(is_grp, logits, -jnp.inf)
    lg_max = jnp.max(lg, axis=-1, keepdims=True)
    g_star = jnp.min(jnp.where(lg == lg_max, lane, LANES), axis=-1, keepdims=True)
    p_grp = 1.0 / jnp.sum(jnp.where(is_grp, jnp.exp(lg - lg_max), 0.0), axis=-1, keepdims=True)
    eid = lane - N_GROUPS
    in_grp = (eid >= g_star * EXP_PER_GROUP) & (eid < (g_star + 1) * EXP_PER_GROUP)
    le = jnp.where(in_grp, logits, -jnp.inf)
    v1 = jnp.max(le, axis=-1, keepdims=True)
    i1 = jnp.min(jnp.where(le == v1, lane, LANES), axis=-1, keepdims=True)
    le2 = jnp.where(lane == i1, -jnp.inf, le)
    v2 = jnp.max(le2, axis=-1, keepdims=True)
    i2 = jnp.min(jnp.where(le2 == v2, lane, LANES), axis=-1, keepdims=True)
    e2 = jnp.exp(v2 - v1)
    w1 = p_grp / (1.0 + e2)
    w2 = p_grp * e2 / (1.0 + e2)
    comb = jnp.where(lane == i1, w1, 0.0) + jnp.where(lane == i2, w2, 0.0)

    xb = xn.astype(BF16)
    cw = _EXP_CHUNK * D_EXPERT
    y = jnp.zeros((tm, D_MODEL), F32)
    for c in range(N_EXPERTS // _EXP_CHUNK):
        hid = _dot(xb, wg_ref[:, c * cw:(c + 1) * cw])
        hid = hid * jax.nn.sigmoid(hid) * _dot(xb, wu_ref[:, c * cw:(c + 1) * cw])
        parts = []
        for j in range(_EXP_CHUNK):
            e_lane = N_GROUPS + c * _EXP_CHUNK + j
            parts.append(hid[:, j * D_EXPERT:(j + 1) * D_EXPERT] * comb[:, e_lane:e_lane + 1])
        hid = jnp.concatenate(parts, axis=1).astype(BF16)
        y = y + _dot(hid, wd_ref[c * cw:(c + 1) * cw, :])
    y_ref[...] = _rms(h + y, gfin_ref[...])


def _moe(h2d, g_ffn, w_rg, b_rg, w_re, b_re, w_gate, w_up, w_down, g_final, tm):
    n = h2d.shape[0]
    assert n % tm == 0
    w_r = jnp.pad(jnp.concatenate([w_rg, w_re], axis=1), ((0, 0), (0, LANES - N_GROUPS - N_EXPERTS)))
    wr_hi, wr_mid, wr_lo = _split3(w_r)
    b_r = jnp.pad(jnp.concatenate([b_rg, b_re]), (0, LANES - N_GROUPS - N_EXPERTS)).reshape(1, LANES)
    wg = jnp.transpose(w_gate, (1, 0, 2)).reshape(D_MODEL, N_EXPERTS * D_EXPERT).astype(BF16)
    wu = jnp.transpose(w_up, (1, 0, 2)).reshape(D_MODEL, N_EXPERTS * D_EXPERT).astype(BF16)
    wd = w_down.reshape(N_EXPERTS * D_EXPERT, D_MODEL).astype(BF16)
    row = pl.BlockSpec((tm, D_MODEL), lambda i: (i, 0))
    full = lambda a: pl.BlockSpec(a.shape, lambda i: (0, 0), pipeline_mode=pl.Buffered(1))
    return pl.pallas_call(
        _moe_body,
        grid=(n // tm,),
        in_specs=[row, full(g_ffn), full(wr_hi), full(wr_mid), full(wr_lo), full(b_r), full(wg), full(wu), full(wd),
                  full(g_final)],
        out_specs=row,
        out_shape=jax.ShapeDtypeStruct((n, D_MODEL), F32),
        compiler_params=_cparams(("parallel",)),
        name="moe",
    )(h2d, g_ffn, wr_hi, wr_mid, wr_lo, b_r, wg, wu, wd, g_final)


def _to_feature_major(x):
    b, t = x.shape[:2]
    return jnp.transpose(x, (0, 2, 3, 1)).reshape(b, KV_W, t)


def _from_feature_major(xt):
    b, _, t = xt.shape
    return jnp.transpose(xt.reshape(b, N_KV, HEAD_DIM, t), (0, 3, 1, 2))


def _layer_prompt(x, l, g_mix, w_in, conv_w, cmp_w, w_nsa_out, w_o, moe_w, g_fin):
    b, t, _ = x.shape
    x2d = x.reshape(b * t, D_MODEL)
    gm = g_mix.reshape(1, D_MODEL)
    (a, qt, qrt, kct, vct, kst, vst, kwt, vwt, ngt, kc_rm, vc_rm, ks_rm, kw_rm, vstb, vwtb) = _proj_in_t(x, gm, w_in, tm=512)
    y_conv = _conv(a.reshape(b, t, C_CONV), None, *conv_w, bb=1, tt=256)
    kc, vc_t = _compress_prompt(kc_rm, vc_rm, b, t, cmp_w)
    o = _nsa_prompt(qt, qrt, ngt, kc, vc_t, ks_rm, kw_rm, vstb, vwtb, b, t, qb=128, kc_len=512)
    h = _merge(x2d, gm, y_conv.reshape(b * t, D_MODEL), o.reshape(b * t, Q_W), w_in, w_nsa_out, w_o, tm=256)
    y = _moe(h, *moe_w, g_fin, tm=256)
    wb = min(WINDOW, t)
    state = (_from_feature_major(kct), _from_feature_major(vct), _from_feature_major(kst), _from_feature_major(vst),
             _from_feature_major(kwt[:, :, t - wb:]), _from_feature_major(vwt[:, :, t - wb:]),
             a.reshape(b, t, C_CONV)[:, t - (CONV_K - 1):])
    return y.reshape(b, t, D_MODEL), state


def _layer_sample(x, ck_cmp, cv_cmp, ck_sel, cv_sel, sk_win, sv_win, s_conv, page_table,
                  g_mix, w_in, conv_w, cmp_w, w_nsa_out, w_o, moe_w, g_fin):
    b, t, _ = x.shape
    past = page_table.shape[1] * PAGE_SIZE
    x2d = x.reshape(b * t, D_MODEL)
    gm = g_mix.reshape(1, D_MODEL)
    a, q, qr, kc_r, vc_r, ks, vs, kw, vw, ng, kvb = _proj_in(x2d, gm, w_in, past + jnp.arange(t), tm=256)
    a3 = a.reshape(b, t, C_CONV)
    hist = jnp.pad(s_conv, ((0, 0), (_HALO - (CONV_K - 1), 0), (0, 0)))
    y_conv = _conv(a3, hist, *conv_w, bb=16, tt=t)
    kc, vc = _compress_sample(_to_feature_major(ck_cmp), _to_feature_major(cv_cmp), page_table, cmp_w)
    sk_t, sv_t = _to_feature_major(sk_win), _to_feature_major(sv_win)
    o = _nsa_sample(q, qr, ng, kc, vc, kvb, sk_t, sv_t, _to_feature_major(ck_sel), _to_feature_major(cv_sel),
                    page_table, tq=t)
    h = _merge(x2d, gm, y_conv.reshape(b * t, D_MODEL), o.reshape(b * t, Q_W), w_in, w_nsa_out, w_o, tm=256)
    y = _moe(h, *moe_w, g_fin, tm=256)
    heads = lambda z: z.reshape(b, t, N_KV, HEAD_DIM)
    k_win = _from_feature_major(jnp.concatenate([sk_t, _to_feature_major(heads(kw))], axis=2)[:, :, t:])
    v_win = _from_feature_major(jnp.concatenate([sv_t, _to_feature_major(heads(vw))], axis=2)[:, :, t:])
    conv_state = jnp.concatenate([s_conv, a3], axis=1)[:, -(CONV_K - 1):]
    state = (heads(kc_r), heads(vc_r), heads(ks), heads(vs), k_win, v_win, conv_state)
    return y.reshape(b, t, D_MODEL), state


def kernel(x_prompt, x_sample, cache_k_cmp, cache_v_cmp, cache_k_sel, cache_v_sel, state_k_win, state_v_win,
           state_conv, page_table, g_mix, w_in, w_dw, b_dw, conv_norm_g, conv_norm_b, w_conv_out,
           w_ck1, b_ck1, w_ck2, w_cv1, b_cv1, w_cv2, w_nsa_out, w_o, g_ffn, w_rg, b_rg, w_re, b_re,
           w_gate, w_up, w_down, g_final):
    depth = g_mix.shape[0]
    assert depth == 1, "single-layer trunk"
    l = 0
    conv_w = (w_dw[l], b_dw[l], conv_norm_g[l], conv_norm_b[l], w_conv_out[l])
    cmp_w = (w_ck1[l], b_ck1[l], w_ck2[l], w_cv1[l], b_cv1[l], w_cv2[l])
    moe_w = (g_ffn[l].reshape(1, D_MODEL), w_rg[l], b_rg[l], w_re[l], b_re[l], w_gate[l], w_up[l], w_down[l])
    g_fin = g_final.reshape(1, D_MODEL)
    y_p, st_p = _layer_prompt(x_prompt, l, g_mix[l], w_in[l], conv_w, cmp_w, w_nsa_out[l], w_o[l], moe_w, g_fin)
    y_s, st_s = _layer_sample(x_sample, cache_k_cmp[l], cache_v_cmp[l], cache_k_sel[l], cache_v_sel[l],
                              state_k_win[l], state_v_win[l], state_conv[l], page_table,
                              g_mix[l], w_in[l], conv_w, cmp_w, w_nsa_out[l], w_o[l], moe_w, g_fin)
    stack = lambda z: z[None]
    return (y_p, y_s) + tuple(stack(z) for z in st_p) + tuple(stack(z) for z in st_s)
```

```python
import functools

import numpy as np
import jax
import jax.numpy as jnp
from jax import lax
from jax.experimental import pallas as pl
from jax.experimental.pallas import tpu as pltpu

D_MODEL = 1024
C_CONV = 512
CONV_K = 31
N_HEADS = 8
N_KV = 2
HEAD_DIM = 64
HPG = N_HEADS // N_KV
CMP_BLOCK = 32
CMP_STRIDE = 16
SEL_BLOCK = 64
N_SEL = 16
WINDOW = 512
PAGE_SIZE = 128
ROPE_THETA = 10000.0
FORCE_BONUS = 1000.0
NEG_INF = -1e30
N_GROUPS = 4
EXP_PER_GROUP = 8
N_EXPERTS = N_GROUPS * EXP_PER_GROUP
D_EXPERT = 128
EPS = 1e-6
KV_W = N_KV * HEAD_DIM
Q_W = N_HEADS * HEAD_DIM
SCALE = HEAD_DIM ** -0.5
HALF = HEAD_DIM // 2
SCALE_LOG2E = SCALE * 1.4426950408889634

LANES = 128
SUBLANES = 8
VMEM_LIMIT = 56 * 1024 * 1024

BF16 = jnp.bfloat16
F32 = jnp.float32

_OFF_U = 0
_OFF_Q = 2 * C_CONV
_OFF_KV = _OFF_Q + Q_W
_OFF_NG = _OFF_KV + 6 * KV_W
_OFF_MG = _OFF_NG + 3 * N_HEADS
_IN_COLS = _OFF_MG + 2 * D_MODEL
_NG_PAD = 32


def _cparams(sem):
    return pltpu.CompilerParams(dimension_semantics=sem, vmem_limit_bytes=VMEM_LIMIT)


def _dot(a, b):
    return jnp.dot(a, b, preferred_element_type=F32)


def _dot_nt(a, b):
    return lax.dot_general(a, b, (((1,), (1,)), ((), ())), preferred_element_type=F32)


def _rms(x, g):
    ms = jnp.mean(x * x, axis=-1, keepdims=True)
    return x * lax.rsqrt(ms + EPS) * g


def _split3(x):
    hi = x.astype(BF16)
    r1 = x - hi.astype(F32)
    mid = r1.astype(BF16)
    lo = (r1 - mid.astype(F32)).astype(BF16)
    return hi, mid, lo


def _rope_tables(pos):
    inv_freq = ROPE_THETA ** (-jnp.arange(HALF, dtype=F32) * 2.0 / HEAD_DIM)
    ang = pos.astype(F32)[:, None] * inv_freq[None, :]
    return jnp.cos(ang), jnp.sin(ang)


def _rope128(x, cos, sin_signed, first_half):
    swapped = jnp.where(first_half, pltpu.roll(x, LANES - HALF, 1), pltpu.roll(x, HALF, 1))
    return x * cos + swapped * sin_signed


def _rope_t(xh, cos, sin):
    x1, x2 = xh[:HALF], xh[HALF:]
    return jnp.concatenate([x1 * cos - x2 * sin, x2 * cos + x1 * sin], axis=0)


def _proj_in_body(x_ref, g_ref, wu_ref, wq_ref, wkv_ref, wng_ref, cos_ref, sin_ref,
                  a_ref, q_ref, qr_ref, kc_ref, vc_ref, ks_ref, vs_ref, kw_ref, vw_ref, ng_ref, kvb_ref):
    xn = _rms(x_ref[...], g_ref[...]).astype(BF16)
    cos = cos_ref[...]
    sin = sin_ref[...]
    lane = lax.broadcasted_iota(jnp.int32, cos.shape, 1)
    first_half = (lane % HEAD_DIM) < HALF

    u = _dot(xn, wu_ref[...])
    a_ref[...] = u[:, :C_CONV] * jax.nn.sigmoid(u[:, C_CONV:])

    q = _dot(xn, wq_ref[...])
    q_ref[...] = (q * SCALE).astype(BF16)
    for c in range(Q_W // LANES):
        qc = q[:, c * LANES:(c + 1) * LANES]
        qr_ref[:, c * LANES:(c + 1) * LANES] = (_rope128(qc, cos, sin, first_half) * SCALE).astype(BF16)

    kv = _dot(xn, wkv_ref[...])
    kc = kv[:, 0 * KV_W:1 * KV_W]
    vc = kv[:, 1 * KV_W:2 * KV_W]
    ks = _rope128(kv[:, 2 * KV_W:3 * KV_W], cos, sin, first_half)
    vs = kv[:, 3 * KV_W:4 * KV_W]
    kw = _rope128(kv[:, 4 * KV_W:5 * KV_W], cos, sin, first_half)
    vw = kv[:, 5 * KV_W:6 * KV_W]
    kc_ref[...] = kc
    vc_ref[...] = vc
    ks_ref[...] = ks
    vs_ref[...] = vs
    kw_ref[...] = kw
    vw_ref[...] = vw
    for i, piece in enumerate((kc, vc, ks, vs, kw, vw)):
        kvb_ref[:, i * KV_W:(i + 1) * KV_W] = piece.astype(BF16)

    ng_ref[...] = jax.nn.sigmoid(_dot(xn, wng_ref[...]))


def _proj_in(x2d, g_mix, w_in, pos_period, tm):
    n = x2d.shape[0]
    assert n % tm == 0
    p = pos_period.shape[0]
    if p < tm:
        assert tm % p == 0
        pos_period = jnp.tile(pos_period, tm // p)
        p = tm
    assert p % tm == 0
    cos, sin = _rope_tables(pos_period)
    cos = jnp.tile(cos, (1, LANES // HALF))
    sin = jnp.tile(jnp.concatenate([-sin, sin], axis=1), (1, LANES // HEAD_DIM))

    wb = w_in.astype(BF16)
    wu = wb[:, _OFF_U:_OFF_Q]
    wq = wb[:, _OFF_Q:_OFF_KV]
    wkv = wb[:, _OFF_KV:_OFF_NG]
    wng = jnp.pad(wb[:, _OFF_NG:_OFF_MG], ((0, 0), (0, LANES - 3 * N_HEADS)))

    nper = p // tm
    row = lambda w: pl.BlockSpec((tm, w), lambda i: (i, 0))
    full = lambda a: pl.BlockSpec(a.shape, lambda i: (0, 0))
    tab = pl.BlockSpec((tm, LANES), lambda i: (i % nper, 0))
    outs = [
        jax.ShapeDtypeStruct((n, C_CONV), F32),
        jax.ShapeDtypeStruct((n, Q_W), BF16),
        jax.ShapeDtypeStruct((n, Q_W), BF16),
    ] + [jax.ShapeDtypeStruct((n, KV_W), F32)] * 6 + [
        jax.ShapeDtypeStruct((n, LANES), F32),
        jax.ShapeDtypeStruct((n, 6 * KV_W), BF16),
    ]
    out_specs = [row(C_CONV), row(Q_W), row(Q_W)] + [row(KV_W)] * 6 + [row(LANES), row(6 * KV_W)]
    return pl.pallas_call(
        _proj_in_body,
        grid=(n // tm,),
        in_specs=[row(D_MODEL), full(g_mix), full(wu), full(wq), full(wkv), full(wng), tab, tab],
        out_specs=out_specs,
        out_shape=outs,
        compiler_params=_cparams(("parallel",)),
        name="proj_in",
    )(x2d, g_mix, wu, wq, wkv, wng, cos, sin)


_ZT_ROWS = Q_W + 6 * KV_W + _NG_PAD


def _proj_in_t_body(x_ref, g_ref, wu_ref, wt_ref, cos_ref, sin_ref,
                    a_ref, qt_ref, qrt_ref, kct_ref, vct_ref, kst_ref, vst_ref, kwt_ref, vwt_ref, ngt_ref,
                    kc_ref, vc_ref, ks_ref, kw_ref, vstb_ref, vwtb_ref, *, tm, nt, nsw):
    xn = _rms(x_ref[...], g_ref[...]).astype(BF16)
    cos = cos_ref[...]
    sin = sin_ref[...]

    u = _dot(xn, wu_ref[...])
    a_ref[...] = u[:, :C_CONV] * jax.nn.sigmoid(u[:, C_CONV:])

    z = _dot_nt(wt_ref[...], xn)
    for h in range(N_HEADS):
        qh = z[h * HEAD_DIM:(h + 1) * HEAD_DIM]
        qt_ref[0, h * HEAD_DIM:(h + 1) * HEAD_DIM, :] = (qh * SCALE_LOG2E).astype(BF16)
        qrt_ref[0, h * HEAD_DIM:(h + 1) * HEAD_DIM, :] = (_rope_t(qh, cos, sin) * SCALE_LOG2E).astype(BF16)

    def kv_rows(i):
        return z[Q_W + i * KV_W:Q_W + (i + 1) * KV_W]

    def rope_groups(x):
        return jnp.concatenate([_rope_t(x[g * HEAD_DIM:(g + 1) * HEAD_DIM], cos, sin) for g in range(N_KV)], axis=0)

    kct, vct, vst, vwt = kv_rows(0), kv_rows(1), kv_rows(3), kv_rows(5)
    kst = rope_groups(kv_rows(2))
    kwt = rope_groups(kv_rows(4))
    kct_ref[0] = kct
    vct_ref[0] = vct
    kst_ref[0] = kst
    vst_ref[0] = vst
    kwt_ref[0] = kwt
    vwt_ref[0] = vwt
    kc_ref[...] = kct.T.astype(BF16)
    vc_ref[...] = vct.T.astype(BF16)
    tok = (pl.program_id(0) % nt) * tm + lax.broadcasted_iota(jnp.int32, (tm, nsw), 0)
    onehot = jnp.where(tok // SEL_BLOCK == lax.broadcasted_iota(jnp.int32, (tm, nsw), 1), 1.0, 0.0).astype(BF16)
    ks_rm = kst.T.astype(BF16)
    for g in range(N_KV):
        ks_ref[:, g * (HEAD_DIM + nsw):g * (HEAD_DIM + nsw) + HEAD_DIM] = ks_rm[:, g * HEAD_DIM:(g + 1) * HEAD_DIM]
        ks_ref[:, g * (HEAD_DIM + nsw) + HEAD_DIM:(g + 1) * (HEAD_DIM + nsw)] = onehot
    kw_ref[...] = kwt.T.astype(BF16)
    for j in range(tm // LANES):
        vstb_ref[0, j] = vst[:, j * LANES:(j + 1) * LANES].astype(BF16)
        vwtb_ref[0, j] = vwt[:, j * LANES:(j + 1) * LANES].astype(BF16)

    ngt_ref[0] = jax.nn.sigmoid(z[Q_W + 6 * KV_W:])


def _proj_in_t(x, g_mix, w_in, tm):
    b, t, _ = x.shape
    assert t % tm == 0 and tm % LANES == 0
    n = b * t
    nt = t // tm
    nsw = _sel_width(t)
    ks_w = N_KV * (HEAD_DIM + nsw)
    cos, sin = _rope_tables(jnp.arange(t))
    cos_t, sin_t = cos.T, sin.T
    wu = w_in[:, _OFF_U:_OFF_Q].astype(BF16)
    wt = jnp.pad(w_in.T[_OFF_Q:_OFF_MG], ((0, _NG_PAD - 3 * N_HEADS), (0, 0))).astype(BF16)

    row = lambda w: pl.BlockSpec((tm, w), lambda i: (i, 0))
    full = lambda a: pl.BlockSpec(a.shape, lambda i: (0, 0))
    tab = pl.BlockSpec((HALF, tm), lambda i: (0, i % nt))
    feat = lambda r: pl.BlockSpec((1, r, tm), lambda i: (i // nt, 0, i % nt))
    chunks = pl.BlockSpec((1, tm // LANES, KV_W, LANES), lambda i: (i // nt, i % nt, 0, 0))
    outs = (
        [jax.ShapeDtypeStruct((n, C_CONV), F32),
         jax.ShapeDtypeStruct((b, Q_W, t), BF16), jax.ShapeDtypeStruct((b, Q_W, t), BF16)]
        + [jax.ShapeDtypeStruct((b, KV_W, t), F32)] * 6
        + [jax.ShapeDtypeStruct((b, _NG_PAD, t), F32)]
        + [jax.ShapeDtypeStruct((n, KV_W), BF16)] * 2
        + [jax.ShapeDtypeStruct((n, ks_w), BF16), jax.ShapeDtypeStruct((n, KV_W), BF16)]
        + [jax.ShapeDtypeStruct((b, t // LANES, KV_W, LANES), BF16)] * 2
    )
    out_specs = ([row(C_CONV), feat(Q_W), feat(Q_W)] + [feat(KV_W)] * 6 + [feat(_NG_PAD)]
                 + [row(KV_W), row(KV_W), row(ks_w), row(KV_W)] + [chunks] * 2)
    return pl.pallas_call(
        functools.partial(_proj_in_t_body, tm=tm, nt=nt, nsw=nsw),
        grid=(n // tm,),
        in_specs=[row(D_MODEL), full(g_mix), full(wu), full(wt), tab, tab],
        out_specs=out_specs,
        out_shape=outs,
        compiler_params=_cparams(("parallel",)),
        name="proj_in_t",
    )(x.reshape(n, D_MODEL), g_mix, wu, wt, cos_t, sin_t)


_HALO = 32


def _conv_body(a_ref, halo_ref, wdw_ref, bdw_ref, ng_ref, nb_ref, wout_ref, y_ref, full_ref, c_ref,
               *, bb, tt, halo_from_prev_tile):
    if halo_from_prev_tile:
        first = pl.program_id(1) == 0

        @pl.when(first)
        def _():
            full_ref[:, 0:_HALO, :] = jnp.zeros((bb, _HALO, C_CONV), F32)

        @pl.when(jnp.logical_not(first))
        def _():
            full_ref[:, 0:_HALO, :] = halo_ref[...]
    else:
        full_ref[:, 0:_HALO, :] = halo_ref[...]
    full_ref[:, _HALO:_HALO + tt, :] = a_ref[...]

    base = _HALO - (CONV_K - 1)
    rc = min(tt, 64)
    for s in range(bb):
        for r0 in range(0, tt, rc):
            for l0 in range(0, C_CONV, LANES):
                acc = jnp.zeros((rc, LANES), F32)
                for k in range(CONV_K):
                    rows = slice(base + r0 + k, base + r0 + k + rc)
                    acc = acc + full_ref[s, rows, l0:l0 + LANES] * wdw_ref[k:k + 1, l0:l0 + LANES]
                c_ref[s * tt + r0:s * tt + r0 + rc, l0:l0 + LANES] = acc + bdw_ref[:, l0:l0 + LANES]

    c = c_ref[...]
    mu = jnp.mean(c, axis=-1, keepdims=True)
    d = c - mu
    var = jnp.mean(d * d, axis=-1, keepdims=True)
    cn = d * lax.rsqrt(var + EPS) * ng_ref[...] + nb_ref[...]
    act = (cn * jax.nn.sigmoid(cn)).astype(BF16)
    y_ref[...] = _dot(act, wout_ref[...]).reshape(bb, tt, D_MODEL)


def _conv(a3, hist, w_dw, b_dw, n_g, n_b, w_out, bb, tt):
    b, t, _ = a3.shape
    assert t % tt == 0 and b % bb == 0 and tt % SUBLANES == 0
    wdw = jnp.pad(w_dw, ((0, 1), (0, 0)))
    wout = w_out.astype(BF16)
    from_prev = hist is None
    if from_prev:
        assert tt % _HALO == 0
        ratio = tt // _HALO
        halo_arr = a3
        halo_spec = pl.BlockSpec((bb, _HALO, C_CONV), lambda bi, i: (bi, jnp.maximum(i * ratio - 1, 0), 0))
    else:
        assert t == tt
        halo_arr = hist
        halo_spec = pl.BlockSpec((bb, _HALO, C_CONV), lambda bi, i: (bi, 0, 0))
    vec = pl.BlockSpec((1, C_CONV), lambda bi, i: (0, 0))
    return pl.pallas_call(
        functools.partial(_conv_body, bb=bb, tt=tt, halo_from_prev_tile=from_prev),
        grid=(b // bb, t // tt),
        in_specs=[
            pl.BlockSpec((bb, tt, C_CONV), lambda bi, i: (bi, i, 0)),
            halo_spec,
            pl.BlockSpec(wdw.shape, lambda bi, i: (0, 0)),
            vec, vec, vec,
            pl.BlockSpec(wout.shape, lambda bi, i: (0, 0)),
        ],
        out_specs=pl.BlockSpec((bb, tt, D_MODEL), lambda bi, i: (bi, i, 0)),
        out_shape=jax.ShapeDtypeStruct((b, t, D_MODEL), F32),
        scratch_shapes=[pltpu.VMEM((bb, _HALO + tt, C_CONV), F32), pltpu.VMEM((bb * tt, C_CONV), F32)],
        compiler_params=_cparams(("parallel", "arbitrary")),
        name="conv",
    )(a3, halo_arr, wdw, b_dw.reshape(1, -1), n_g.reshape(1, -1), n_b.reshape(1, -1), wout)


def _compress_weights(w1, b1, w2):
    eye = jnp.eye(N_KV, dtype=F32)
    big = jnp.einsum("gh,lde->lgdhe", eye, w1).reshape(CMP_BLOCK, KV_W, KV_W)
    wa = big[:CMP_STRIDE].reshape(CMP_STRIDE * KV_W, KV_W)
    wb = big[CMP_STRIDE:].reshape(CMP_STRIDE * KV_W, KV_W)
    wcat = jnp.concatenate([wa, wb], axis=1).astype(BF16)
    w2bd = jnp.einsum("gh,ef->gehf", eye, w2).reshape(KV_W, KV_W)
    b1t = jnp.tile(b1, N_KV).reshape(1, KV_W)
    return wcat, b1t, w2bd.astype(BF16), w2bd.T.astype(BF16)


def _compress_hidden(p, b1_ref):
    s = p.shape[0]
    pre = p[:, :KV_W] + pltpu.roll(p[:, KV_W:], s - 1, 0) + b1_ref[...]
    return (pre * jax.nn.sigmoid(pre)).astype(BF16)


def _compress_prompt_body(k_ref, v_ref, wk_ref, bk_ref, wk2_ref, wv_ref, bv_ref, wv2t_ref, kc_ref, vct_ref):
    hid_k = _compress_hidden(_dot(k_ref[0], wk_ref[...]), bk_ref)
    kc_ref[0] = _dot(hid_k, wk2_ref[...]).astype(BF16)
    hid_v = _compress_hidden(_dot(v_ref[0], wv_ref[...]), bv_ref)
    vct_ref[0] = _dot_nt(wv2t_ref[...], hid_v).astype(BF16)


def _compress_prompt(kc_rm, vc_rm, b, t, cmp_w):
    s = t // CMP_STRIDE
    feat = CMP_STRIDE * KV_W
    k_rows = kc_rm.reshape(b, s, feat)
    v_rows = vc_rm.reshape(b, s, feat)
    wk, bk, wk2, _ = _compress_weights(*cmp_w[:3])
    wv, bv, _, wv2t = _compress_weights(*cmp_w[3:])
    rows = pl.BlockSpec((1, s, feat), lambda i: (i, 0, 0))
    full = lambda a: pl.BlockSpec(a.shape, lambda i: (0, 0))
    return pl.pallas_call(
        _compress_prompt_body,
        grid=(b,),
        in_specs=[rows, rows, full(wk), full(bk), full(wk2), full(wv), full(bv), full(wv2t)],
        out_specs=[pl.BlockSpec((1, s, KV_W), lambda i: (i, 0, 0)), pl.BlockSpec((1, KV_W, s), lambda i: (i, 0, 0))],
        out_shape=[jax.ShapeDtypeStruct((b, s, KV_W), BF16), jax.ShapeDtypeStruct((b, KV_W, s), BF16)],
        compiler_params=_cparams(("parallel",)),
        name="compress_prompt",
    )(k_rows, v_rows, wk, bk, wk2, wv, bv, wv2t)


def _gather_pages(pools, bufs, sems, pt_ref, seq, slot, n_pages, dst_of, wait):
    def body(j, carry):
        page = pt_ref[seq * n_pages + j]
        for pool_ref, buf_ref, sem_ref in zip(pools, bufs, sems):
            cp = pltpu.make_async_copy(pool_ref.at[page], dst_of(buf_ref, slot, j), sem_ref.at[slot])
            if wait:
                cp.wait()
            else:
                cp.start()
        return carry
    lax.fori_loop(0, n_pages, body, 0)


def _paged_prologue(pools, bufs, sems, pt_ref, n_pages, dst_of):
    i = pl.program_id(0)
    n = pl.num_programs(0)
    slot = i % 2

    @pl.when(i == 0)
    def _():
        _gather_pages(pools, bufs, sems, pt_ref, i, slot, n_pages, dst_of, wait=False)

    @pl.when(i + 1 < n)
    def _():
        _gather_pages(pools, bufs, sems, pt_ref, i + 1, 1 - slot, n_pages, dst_of, wait=False)

    _gather_pages(pools, bufs, sems, pt_ref, i, slot, n_pages, dst_of, wait=True)
    return slot


def _dst_page_major(buf_ref, slot, j):
    return buf_ref.at[slot, j]


def _dst_token_lanes(buf_ref, slot, j):
    return buf_ref.at[slot, :, pl.ds(pl.multiple_of(j * PAGE_SIZE, PAGE_SIZE), PAGE_SIZE)]


def _compress_sample_body(pt_ref, kpool_ref, vpool_ref, perm_ref, wk_ref, bk_ref, wk2_ref, wv_ref, bv_ref, wv2_ref,
                          kc_ref, vc_ref, kbuf, vbuf, ksem, vsem, tok_ref, *, n_pages):
    slot = _paged_prologue((kpool_ref, vpool_ref), (kbuf, vbuf), (ksem, vsem), pt_ref, n_pages, _dst_page_major)
    spp = PAGE_SIZE // CMP_STRIDE
    n_rows = n_pages * spp

    def compress(buf, w_ref, b_ref, w2_ref, out_ref):
        def tr(j, carry):
            tok_ref[j] = _dot_nt(perm_ref[...], buf[slot, j].astype(BF16))
            return carry
        lax.fori_loop(0, n_pages, tr, 0, unroll=16)
        p = jnp.zeros((n_rows, 2 * KV_W), F32)
        for l in range(0, CMP_STRIDE, 2):
            y = jnp.concatenate(
                [tok_ref[:, (l + u) * spp:(l + u + 1) * spp, :].reshape(n_rows, KV_W) for u in range(2)], axis=1)
            p = p + _dot(y.astype(BF16), w_ref[l * KV_W:(l + 2) * KV_W, :])
        out_ref[0] = _dot(_compress_hidden(p, b_ref), w2_ref[...]).astype(BF16)

    compress(kbuf, wk_ref, bk_ref, wk2_ref, kc_ref)
    compress(vbuf, wv_ref, bv_ref, wv2_ref, vc_ref)


def _compress_sample(pool_k_t, pool_v_t, page_table, cmp_w):
    b, n_pages = page_table.shape
    s = n_pages * PAGE_SIZE // CMP_STRIDE
    wk, bk, wk2, _ = _compress_weights(*cmp_w[:3])
    wv, bv, wv2, _ = _compress_weights(*cmp_w[3:])
    spp = PAGE_SIZE // CMP_STRIDE
    r = np.arange(PAGE_SIZE)
    perm = jnp.asarray(((r % spp) * CMP_STRIDE + r // spp)[:, None] == r[None, :], dtype=BF16)
    full = lambda a: pl.BlockSpec(a.shape, lambda i, pt: (0, 0))
    out = pl.BlockSpec((1, s, KV_W), lambda i, pt: (i, 0, 0))
    anyspec = pl.BlockSpec(memory_space=pl.ANY)
    grid_spec = pltpu.PrefetchScalarGridSpec(
        num_scalar_prefetch=1,
        grid=(b,),
        in_specs=[anyspec, anyspec, full(perm), full(wk), full(bk), full(wk2), full(wv), full(bv), full(wv2)],
        out_specs=[out, out],
        scratch_shapes=[
            pltpu.VMEM((2, n_pages, KV_W, PAGE_SIZE), F32),
            pltpu.VMEM((2, n_pages, KV_W, PAGE_SIZE), F32),
            pltpu.SemaphoreType.DMA((2,)),
            pltpu.SemaphoreType.DMA((2,)),
            pltpu.VMEM((n_pages, PAGE_SIZE, KV_W), F32),
        ],
    )
    return pl.pallas_call(
        functools.partial(_compress_sample_body, n_pages=n_pages),
        grid_spec=grid_spec,
        out_shape=[jax.ShapeDtypeStruct((b, s, KV_W), BF16)] * 2,
        compiler_params=_cparams(("arbitrary",)),
        name="compress_sample",
    )(page_table.reshape(-1), pool_k_t, pool_v_t, perm, wk, bk, wk2, wv, bv, wv2)


def _overlap_np(nc_pad, ns_pad):
    i = np.arange(nc_pad)[:, None]
    j = np.arange(ns_pad)[None, :]
    return ((i * CMP_STRIDE < (j + 1) * SEL_BLOCK) & (i * CMP_STRIDE + CMP_BLOCK > j * SEL_BLOCK)).astype(np.float32)


def _importance_bonus(imp, j, t):
    cur = t // SEL_BLOCK
    forced = (j == 0) | (j == cur) | (j == cur - 1)
    blk_ok = j * SEL_BLOCK <= t
    return jnp.where(blk_ok, imp + jnp.where(forced, FORCE_BONUS, 0.0), NEG_INF)


def _sel_width(t):
    return -(-max(t // SEL_BLOCK, 1) // HEAD_DIM) * HEAD_DIM


def _softmax2_cols(s):
    m = jnp.max(s, axis=0, keepdims=True)
    e = jnp.exp2(s - m)
    return e / jnp.sum(e, axis=0, keepdims=True)


def _top_k_cols(imp, k):
    ns = imp.shape[0]
    j = lax.broadcasted_iota(jnp.int32, imp.shape, 0)
    cnt = jnp.zeros(imp.shape, F32)
    for i in range(ns):
        row = imp[i:i + 1, :]
        tie = jnp.where(j > i, 1.0, 0.0)
        cnt = cnt + jnp.where(row > imp, 1.0, jnp.where(row == imp, tie, 0.0))
    return jnp.where(cnt < k, 1.0, 0.0)


def _heads_on_lanes(xt, g, rows_per_head):
    return jnp.concatenate(
        [xt[(g * HPG + h) * rows_per_head:(g * HPG + h + 1) * rows_per_head] for h in range(HPG)], axis=1)


def _gate_row(ngt, g, branch):
    return jnp.concatenate([ngt[(g * HPG + h) * 3 + branch:(g * HPG + h) * 3 + branch + 1] for h in range(HPG)], axis=1)


def _nsa_prompt_body(qt_ref, qrt_ref, ngt_ref, kc_ref, vct_ref, ks_ref, kw_ref, vst_ref, vwt_ref, ovt_ref,
                     o_ref, *, qb, kc_len, nsw):
    i = pl.program_id(1)
    start = i * qb
    t_row = start + lax.broadcasted_iota(jnp.int32, (1, qb), 1)
    t4 = jnp.concatenate([t_row] * HPG, axis=1)
    qt = qt_ref[0]
    qrt = qrt_ref[0]
    ngt = ngt_ref[0]
    nc = kc_ref.shape[1]
    ns = ovt_ref.shape[0]
    n_chunks = (start + qb + kc_len - 1) // kc_len
    sub = kc_len // LANES
    w_len = WINDOW + qb
    w_start = pl.multiple_of(jnp.maximum(start - WINDOW, 0), qb)
    w_blk = w_start // LANES

    groups = range(N_KV)
    glanes = [slice(g * HEAD_DIM, (g + 1) * HEAD_DIM) for g in groups]
    qr = [_heads_on_lanes(qrt, g, HEAD_DIM) for g in groups]
    kw_aug = HEAD_DIM + nsw
    j_blk = lax.broadcasted_iota(jnp.int32, (ns, qb), 0)
    c_end = lax.broadcasted_iota(jnp.int32, (nc, 1), 0) * CMP_STRIDE + (CMP_BLOCK - 1)
    ovt = ovt_ref[...]

    o_c, rhs = [], []
    for g in groups:
        s_c = _dot(kc_ref[0][:, glanes[g]], _heads_on_lanes(qt, g, HEAD_DIM))
        p_c = _softmax2_cols(jnp.where(c_end <= t4, s_c, NEG_INF))
        p_c = p_c * jnp.where(t4 >= CMP_BLOCK - 1, 1.0, 0.0)
        o_c.append(_dot(vct_ref[0][glanes[g], :], p_c.astype(BF16)))
        p_sum = p_c[:, 0:qb] + p_c[:, qb:2 * qb] + p_c[:, 2 * qb:3 * qb] + p_c[:, 3 * qb:4 * qb]
        hi, mid, lo = _split3(p_sum)
        imp = _dot(ovt, hi) + _dot(ovt, mid) + _dot(ovt, lo)
        sel = _top_k_cols(_importance_bonus(imp, j_blk, t_row), min(N_SEL, ns))
        open_blk = (sel > 0.5) & (j_blk * SEL_BLOCK <= t_row)
        blk_bias = jnp.where(open_blk, 0.0, NEG_INF)
        if nsw > ns:
            blk_bias = jnp.concatenate([blk_bias, jnp.zeros((nsw - ns, qb), F32)], axis=0)
        rhs.append(jnp.concatenate([qr[g], jnp.concatenate([blk_bias.astype(BF16)] * HPG, axis=1)], axis=0))

    def chunk(c, carry, causal):
        k0 = pl.multiple_of(c * kc_len, kc_len)
        vblk = vst_ref[0, pl.ds(c * sub, sub)]
        if causal:
            key = k0 + lax.broadcasted_iota(jnp.int32, (kc_len, 1), 0)
            causal_bias = jnp.concatenate([jnp.where(key <= t_row, 0.0, NEG_INF)] * HPG, axis=1)
        out = []
        for g in groups:
            m, l, acc = carry[g]
            kch = ks_ref[0, pl.ds(k0, kc_len), g * kw_aug:(g + 1) * kw_aug]
            vch = jnp.concatenate([vblk[u][glanes[g], :] for u in range(sub)], axis=1)
            s = _dot(kch, rhs[g])
            if causal:
                s = s + causal_bias
            m_new = jnp.maximum(m, jnp.max(s, axis=0, keepdims=True))
            alpha = jnp.exp2(m - m_new)
            p = jnp.exp2(s - m_new)
            l = alpha * l + jnp.sum(p, axis=0, keepdims=True)
            acc = alpha * acc + _dot(vch, p.astype(BF16))
            out.append((m_new, l, acc))
        return tuple(out)

    init = tuple((jnp.full((1, HPG * qb), NEG_INF, F32), jnp.zeros((1, HPG * qb), F32),
                  jnp.zeros((HEAD_DIM, HPG * qb), F32)) for _ in groups)
    carry = lax.fori_loop(0, n_chunks - 1, functools.partial(chunk, causal=False), init)
    carry = chunk(n_chunks - 1, carry, causal=True)

    wblk = vwt_ref[0, pl.ds(w_blk, w_len // LANES)]
    key = w_start + lax.broadcasted_iota(jnp.int32, (w_len, 1), 0)
    ok = (key <= t4) & (t4 - key < WINDOW)
    o_heads = []
    for g in groups:
        _, l, acc = carry[g]
        kwc = kw_ref[0, pl.ds(w_start, w_len), glanes[g]]
        vwc = jnp.concatenate([wblk[u][glanes[g], :] for u in range(w_len // LANES)], axis=1)
        p_w = _softmax2_cols(jnp.where(ok, _dot(kwc, qr[g]), NEG_INF))
        o_w = _dot(vwc, p_w.astype(BF16))
        o = _gate_row(ngt, g, 0) * o_c[g] + _gate_row(ngt, g, 1) * (acc / l) + _gate_row(ngt, g, 2) * o_w
        o_heads += [o[:, h * qb:(h + 1) * qb] for h in range(HPG)]
    o_ref[0] = jnp.concatenate(o_heads, axis=0).T.astype(o_ref.dtype)


def _nsa_prompt(qt, qrt, ngt, kc, vct, ks_aug, kw_rm, vstb, vwtb, b, t, qb, kc_len):
    assert t % kc_len == 0 and kc_len % qb == 0 and t >= WINDOW + qb and WINDOW % qb == 0 and qb % LANES == 0
    ns = t // SEL_BLOCK
    nsw = _sel_width(t)
    ks_w = N_KV * (HEAD_DIM + nsw)
    nc_pad = t // CMP_STRIDE
    ovt = jnp.asarray(_overlap_np(nc_pad, ns).T, dtype=BF16)
    tile = lambda r: pl.BlockSpec((1, r, qb), lambda bi, i: (bi, 0, i))
    per_b = lambda shp: pl.BlockSpec((1,) + shp, lambda bi, i: (bi,) + (0,) * len(shp))
    const = lambda a: pl.BlockSpec(a.shape, lambda bi, i: (0,) * a.ndim)
    return pl.pallas_call(
        functools.partial(_nsa_prompt_body, qb=qb, kc_len=kc_len, nsw=nsw),
        grid=(b, t // qb),
        in_specs=[
            tile(Q_W), tile(Q_W), tile(_NG_PAD),
            per_b((nc_pad, KV_W)), per_b((KV_W, nc_pad)),
            per_b((t, ks_w)), per_b((t, KV_W)),
            per_b((t // LANES, KV_W, LANES)), per_b((t // LANES, KV_W, LANES)),
            const(ovt),
        ],
        out_specs=pl.BlockSpec((1, qb, Q_W), lambda bi, i: (bi, i, 0)),
        out_shape=jax.ShapeDtypeStruct((b, t, Q_W), BF16),
        compiler_params=_cparams(("parallel", "arbitrary")),
        name="nsa_prompt",
    )(qt, qrt, ngt, kc, vct, ks_aug.reshape(b, t, ks_w), kw_rm.reshape(b, t, KV_W), vstb, vwtb, ovt)


def _block_diag_q(q, tq):
    z = jnp.zeros((HPG * tq, HEAD_DIM), q.dtype)
    rows = []
    for g in range(N_KV):
        qs = jnp.concatenate([q[:, (g * HPG + h) * HEAD_DIM:(g * HPG + h + 1) * HEAD_DIM] for h in range(HPG)], axis=0)
        rows.append(jnp.concatenate([qs, z] if g == 0 else [z, qs], axis=1))
    return jnp.concatenate(rows, axis=0)


def _rows_ghq(x, tq):
    return jnp.concatenate([x[g * tq:(g + 1) * tq] for g in range(N_KV) for _ in range(HPG)], axis=0)


def _top_k_rows(imp, imp_t, k, ni):
    r, ns = imp.shape
    ii = lax.broadcasted_iota(jnp.int32, (ni, ns), 0)
    jj = lax.broadcasted_iota(jnp.int32, (ni, ns), 1)
    tie = jnp.where(ii < jj, 1.0, 0.0)
    out = []
    for x in range(r):
        col = imp_t[:ni, x:x + 1]
        row = imp[x:x + 1, :]
        beats = jnp.where(col > row, 1.0, jnp.where(col == row, tie, 0.0))
        out.append(jnp.sum(beats, axis=0, keepdims=True))
    return jnp.where(jnp.concatenate(out, axis=0) < k, 1.0, 0.0)


def _nsa_sample_body(pt_ref, q_ref, qr_ref, ng_ref, kc_ref, vc_ref, kvn_ref, kwin_ref, vwin_ref, kpool_ref, vpool_ref,
                     ov_ref, e_ref, o_ref, kbuf, vbuf, ksem, vsem, kcat_ref, *, n_pages, tq):
    slot = _paged_prologue((kpool_ref, vpool_ref), (kbuf, vbuf), (ksem, vsem), pt_ref, n_pages, _dst_token_lanes)
    past = n_pages * PAGE_SIZE
    ns_past = past // SEL_BLOCK
    nrow = N_KV * HPG * tq
    wb = kwin_ref.shape[2]

    @pl.when(pl.program_id(0) == 0)
    def _():
        kcat_ref[KV_W:, :] = e_ref[...]

    t_q = past + lax.broadcasted_iota(jnp.int32, (tq, 1), 0)
    t16 = jnp.concatenate([t_q] * N_KV, axis=0)
    t_rows = _rows_ghq(t16, tq)
    row_g = lax.broadcasted_iota(jnp.int32, (nrow, 1), 0) // (HPG * tq)

    def own_group(x):
        return jnp.where(row_g == 0, x[:, :HEAD_DIM], x[:, HEAD_DIM:])

    q_bd = _block_diag_q(q_ref[0], tq)
    qr_bd = _block_diag_q(qr_ref[0], tq)
    kvn = kvn_ref[0]
    new = lambda idx: kvn[:, idx * KV_W:(idx + 1) * KV_W]

    nc = kc_ref.shape[1]
    s_c = _dot_nt(q_bd, kc_ref[0])
    c_end = lax.broadcasted_iota(jnp.int32, (1, nc), 1) * CMP_STRIDE + (CMP_BLOCK - 1)
    s_c = jnp.where(c_end <= t_rows, s_c, NEG_INF)
    m_c = jnp.max(s_c, axis=-1, keepdims=True)
    e_c = jnp.exp(s_c - m_c)
    p_c = e_c / jnp.sum(e_c, axis=-1, keepdims=True) * jnp.where(t_rows >= CMP_BLOCK - 1, 1.0, 0.0)
    o_c = own_group(_dot(p_c.astype(BF16), vc_ref[0]))
    p_sum = jnp.concatenate(
        [sum(p_c[(g * HPG + h) * tq:(g * HPG + h + 1) * tq] for h in range(HPG)) for g in range(N_KV)], axis=0)
    hi, mid, lo = _split3(p_sum)
    ov = ov_ref[...]
    imp = _dot(hi, ov) + _dot(mid, ov) + _dot(lo, ov)
    ns_pad = imp.shape[1]
    j_blk = lax.broadcasted_iota(jnp.int32, imp.shape, 1)
    imp = _importance_bonus(imp, j_blk, t16)
    imp_sq = jnp.concatenate([imp, jnp.zeros((ns_pad - N_KV * tq, ns_pad), F32)], axis=0)
    ni = -(-(ns_past + 1) // SUBLANES) * SUBLANES
    sel = _top_k_rows(imp, imp_sq.T, min(N_SEL, ns_past + 1), ni)

    bias = jnp.where(sel[:, :ns_past] > 0.5, 0.0, NEG_INF)
    lhs = jnp.concatenate([qr_bd, _rows_ghq(bias, tq).astype(BF16)], axis=1)
    kcat_ref[:KV_W, :] = kbuf[slot].astype(BF16)
    s_past = _dot(lhs, kcat_ref[...])
    new_sel = _rows_ghq(sel[:, ns_past:ns_past + 1], tq) > 0.5
    key_new = past + lax.broadcasted_iota(jnp.int32, (1, tq), 1)
    causal_new = key_new <= t_rows
    s_new = jnp.where(new_sel & causal_new, _dot_nt(qr_bd, new(2)), NEG_INF)
    m = jnp.maximum(jnp.max(s_past, axis=-1, keepdims=True), jnp.max(s_new, axis=-1, keepdims=True))
    p_past = jnp.exp(s_past - m)
    p_new = jnp.exp(s_new - m)
    l = jnp.sum(p_past, axis=-1, keepdims=True) + jnp.sum(p_new, axis=-1, keepdims=True)
    o_s = own_group(_dot_nt(p_past.astype(BF16), vbuf[slot].astype(BF16)) + _dot(p_new.astype(BF16), new(3))) / l

    pos_p = (past - wb) + lax.broadcasted_iota(jnp.int32, (1, wb), 1)
    ok_p = (pos_p <= t_rows) & (t_rows - pos_p < WINDOW) & (pos_p >= 0)
    sw_p = jnp.where(ok_p, _dot(qr_bd, kwin_ref[0].astype(BF16)), NEG_INF)
    sw_n = jnp.where(causal_new, _dot_nt(qr_bd, new(4)), NEG_INF)
    mw = jnp.maximum(jnp.max(sw_p, axis=-1, keepdims=True), jnp.max(sw_n, axis=-1, keepdims=True))
    pw_p = jnp.exp(sw_p - mw)
    pw_n = jnp.exp(sw_n - mw)
    lw = jnp.sum(pw_p, axis=-1, keepdims=True) + jnp.sum(pw_n, axis=-1, keepdims=True)
    o_w = own_group(_dot_nt(pw_p.astype(BF16), vwin_ref[0].astype(BF16)) + _dot(pw_n.astype(BF16), new(5))) / lw

    ng = ng_ref[0]
    gate = lambda br: jnp.concatenate([ng[:, hh * 3 + br:hh * 3 + br + 1] for hh in range(N_HEADS)], axis=0)
    o = gate(0) * o_c + gate(1) * o_s + gate(2) * o_w
    for hh in range(N_HEADS):
        o_ref[0, :, hh * HEAD_DIM:(hh + 1) * HEAD_DIM] = o[hh * tq:(hh + 1) * tq].astype(o_ref.dtype)


def _nsa_sample(q, qr, ng, kc, vc, kvb, k_win_t, v_win_t, pool_k_t, pool_v_t, page_table, tq):
    b, n_pages = page_table.shape
    past = n_pages * PAGE_SIZE
    assert tq < CMP_STRIDE and tq <= SEL_BLOCK and past % SEL_BLOCK == 0 and tq % SUBLANES == 0
    wb = k_win_t.shape[2]
    ns_past = past // SEL_BLOCK
    ns_pad = (ns_past + 1 + LANES - 1) // LANES * LANES
    nc_pad = past // CMP_STRIDE
    ov = jnp.asarray(_overlap_np(nc_pad, ns_pad), dtype=BF16)
    blk = np.arange(past) // SEL_BLOCK
    e = jnp.asarray((np.arange(ns_past)[:, None] == blk[None, :]).astype(np.float32), dtype=BF16)
    per_b = lambda rows, w: pl.BlockSpec((1, rows, w), lambda i, pt: (i, 0, 0))
    anyspec = pl.BlockSpec(memory_space=pl.ANY)
    grid_spec = pltpu.PrefetchScalarGridSpec(
        num_scalar_prefetch=1,
        grid=(b,),
        in_specs=[
            per_b(tq, Q_W), per_b(tq, Q_W), per_b(tq, LANES),
            per_b(nc_pad, KV_W), per_b(nc_pad, KV_W), per_b(tq, 6 * KV_W),
            per_b(KV_W, wb), per_b(KV_W, wb),
            anyspec, anyspec,
            pl.BlockSpec(ov.shape, lambda i, pt: (0, 0)),
            pl.BlockSpec(e.shape, lambda i, pt: (0, 0)),
        ],
        out_specs=per_b(tq, Q_W),
        scratch_shapes=[
            pltpu.VMEM((2, KV_W, past), F32),
            pltpu.VMEM((2, KV_W, past), F32),
            pltpu.SemaphoreType.DMA((2,)),
            pltpu.SemaphoreType.DMA((2,)),
            pltpu.VMEM((KV_W + ns_past, past), BF16),
        ],
    )
    return pl.pallas_call(
        functools.partial(_nsa_sample_body, n_pages=n_pages, tq=tq),
        grid_spec=grid_spec,
        out_shape=jax.ShapeDtypeStruct((b, tq, Q_W), BF16),
        compiler_params=_cparams(("arbitrary",)),
        name="nsa_sample",
    )(page_table.reshape(-1), q.reshape(b, tq, Q_W), qr.reshape(b, tq, Q_W), ng.reshape(b, tq, LANES), kc, vc,
      kvb.reshape(b, tq, 6 * KV_W), k_win_t, v_win_t, pool_k_t, pool_v_t, ov, e)


def _merge_body(x_ref, g_ref, yc_ref, o_ref, wmg_ref, wn_ref, wo_ref, h_ref):
    x = x_ref[...]
    xn = _rms(x, g_ref[...]).astype(BF16)
    mg = jax.nn.sigmoid(_dot(xn, wmg_ref[...]))
    y_nsa = _dot(o_ref[...], wn_ref[...])
    mix = mg[:, :D_MODEL] * yc_ref[...] + mg[:, D_MODEL:] * y_nsa
    h_ref[...] = x + _dot(mix.astype(BF16), wo_ref[...])


def _merge(x2d, g_mix, y_conv, o, w_in, w_nsa_out, w_o, tm):
    n = x2d.shape[0]
    assert n % tm == 0
    wmg = w_in[:, _OFF_MG:_IN_COLS].astype(BF16)
    wn = w_nsa_out.astype(BF16)
    wo = w_o.astype(BF16)
    row = lambda w: pl.BlockSpec((tm, w), lambda i: (i, 0))
    full = lambda a: pl.BlockSpec(a.shape, lambda i: (0, 0))
    return pl.pallas_call(
        _merge_body,
        grid=(n // tm,),
        in_specs=[row(D_MODEL), full(g_mix), row(D_MODEL), row(Q_W), full(wmg), full(wn), full(wo)],
        out_specs=row(D_MODEL),
        out_shape=jax.ShapeDtypeStruct((n, D_MODEL), F32),
        compiler_params=_cparams(("parallel",)),
        name="merge",
    )(x2d, g_mix, y_conv, o, wmg, wn, wo)


_EXP_CHUNK = 2


def _moe_body(h_ref, gf_ref, wr_hi_ref, wr_mid_ref, wr_lo_ref, br_ref, wg_ref, wu_ref, wd_ref, gfin_ref, y_ref):
    h = h_ref[...]
    xn = _rms(h, gf_ref[...])
    tm = h.shape[0]

    x_hi, x_mid, x_lo = _split3(xn)
    w_hi, w_mid, w_lo = wr_hi_ref[...], wr_mid_ref[...], wr_lo_ref[...]
    logits = (_dot(x_hi, w_hi) + (_dot(x_hi, w_mid) + _dot(x_mid, w_hi))
              + (_dot(x_hi, w_lo) + _dot(x_mid, w_mid) + _dot(x_lo, w_hi))) + br_ref[...]
    lane = lax.broadcasted_iota(jnp.int32, logits.shape, 1)
    is_grp = lane < N_GROUPS
    lg = jnp.where(is_grp, logits, -jnp.inf)
    lg_max = jnp.max(lg, axis=-1, keepdims=True)
    g_star = jnp.min(jnp.where(lg == lg_max, lane, LANES), axis=-1, keepdims=True)
    p_grp = 1.0 / jnp.sum(jnp.where(is_grp, jnp.exp(lg - lg_max), 0.0), axis=-1, keepdims=True)
    eid = lane - N_GROUPS
    in_grp = (eid >= g_star * EXP_PER_GROUP) & (eid < (g_star + 1) * EXP_PER_GROUP)
    le = jnp.where(in_grp, logits, -jnp.inf)
    v1 = jnp.max(le, axis=-1, keepdims=True)
    i1 = jnp.min(jnp.where(le == v1, lane, LANES), axis=-1, keepdims=True)
    le2 = jnp.where(lane == i1, -jnp.inf, le)
    v2 = jnp.max(le2, axis=-1, keepdims=True)
    i2 = jnp.min(jnp.where(le2 == v2, lane, LANES), axis=-1, keepdims=True)
    e2 = jnp.exp(v2 - v1)
    w1 = p_grp / (1.0 + e2)
    w2 = p_grp * e2 / (1.0 + e2)
    comb = jnp.where(lane == i1, w1, 0.0) + jnp.where(lane == i2, w2, 0.0)

    xb = xn.astype(BF16)
    cw = _EXP_CHUNK * D_EXPERT
    y = jnp.zeros((tm, D_MODEL), F32)
    for c in range(N_EXPERTS // _EXP_CHUNK):
        hid = _dot(xb, wg_ref[:, c * cw:(c + 1) * cw])
        hid = hid * jax.nn.sigmoid(hid) * _dot(xb, wu_ref[:, c * cw:(c + 1) * cw])
        parts = []
        for j in range(_EXP_CHUNK):
            e_lane = N_GROUPS + c * _EXP_CHUNK + j
            parts.append(hid[:, j * D_EXPERT:(j + 1) * D_EXPERT] * comb[:, e_lane:e_lane + 1])
        hid = jnp.concatenate(parts, axis=1).astype(BF16)
        y = y + _dot(hid, wd_ref[c * cw:(c + 1) * cw, :])
    y_ref[...] = _rms(h + y, gfin_ref[...])


def _moe(h2d, g_ffn, w_rg, b_rg, w_re, b_re, w_gate, w_up, w_down, g_final, tm):
    n = h2d.shape[0]
    assert n % tm == 0
    w_r = jnp.pad(jnp.concatenate([w_rg, w_re], axis=1), ((0, 0), (0, LANES - N_GROUPS - N_EXPERTS)))
    wr_hi, wr_mid, wr_lo = _split3(w_r)
    b_r = jnp.pad(jnp.concatenate([b_rg, b_re]), (0, LANES - N_GROUPS - N_EXPERTS)).reshape(1, LANES)
    wg = jnp.transpose(w_gate, (1, 0, 2)).reshape(D_MODEL, N_EXPERTS * D_EXPERT).astype(BF16)
    wu = jnp.transpose(w_up, (1, 0, 2)).reshape(D_MODEL, N_EXPERTS * D_EXPERT).astype(BF16)
    wd = w_down.reshape(N_EXPERTS * D_EXPERT, D_MODEL).astype(BF16)
    row = pl.BlockSpec((tm, D_MODEL), lambda i: (i, 0))
    full = lambda a: pl.BlockSpec(a.shape, lambda i: (0, 0), pipeline_mode=pl.Buffered(1))
    return pl.pallas_call(
        _moe_body,
        grid=(n // tm,),
        in_specs=[row, full(g_ffn), full(wr_hi), full(wr_mid), full(wr_lo), full(b_r), full(wg), full(wu), full(wd),
                  full(g_final)],
        out_specs=row,
        out_shape=jax.ShapeDtypeStruct((n, D_MODEL), F32),
        compiler_params=_cparams(("parallel",)),
        name="moe",
    )(h2d, g_ffn, wr_hi, wr_mid, wr_lo, b_r, wg, wu, wd, g_final)


def _to_feature_major(x):
    b, t = x.shape[:2]
    return jnp.transpose(x, (0, 2, 3, 1)).reshape(b, KV_W, t)


def _from_feature_major(xt):
    b, _, t = xt.shape
    return jnp.transpose(xt.reshape(b, N_KV, HEAD_DIM, t), (0, 3, 1, 2))


def _layer_prompt(x, l, g_mix, w_in, conv_w, cmp_w, w_nsa_out, w_o, moe_w, g_fin):
    b, t, _ = x.shape
    x2d = x.reshape(b * t, D_MODEL)
    gm = g_mix.reshape(1, D_MODEL)
    (a, qt, qrt, kct, vct, kst, vst, kwt, vwt, ngt, kc_rm, vc_rm, ks_aug, kw_rm, vstb, vwtb) = _proj_in_t(x, gm, w_in, tm=512)
    y_conv = _conv(a.reshape(b, t, C_CONV), None, *conv_w, bb=1, tt=256)
    kc, vc_t = _compress_prompt(kc_rm, vc_rm, b, t, cmp_w)
    o = _nsa_prompt(qt, qrt, ngt, kc, vc_t, ks_aug, kw_rm, vstb, vwtb, b, t, qb=128, kc_len=512)
    h = _merge(x2d, gm, y_conv.reshape(b * t, D_MODEL), o.reshape(b * t, Q_W), w_in, w_nsa_out, w_o, tm=256)
    y = _moe(h, *moe_w, g_fin, tm=256)
    wb = min(WINDOW, t)
    state = (_from_feature_major(kct), _from_feature_major(vct), _from_feature_major(kst), _from_feature_major(vst),
             _from_feature_major(kwt[:, :, t - wb:]), _from_feature_major(vwt[:, :, t - wb:]),
             a.reshape(b, t, C_CONV)[:, t - (CONV_K - 1):])
    return y.reshape(b, t, D_MODEL), state


def _layer_sample(x, ck_cmp, cv_cmp, ck_sel, cv_sel, sk_win, sv_win, s_conv, page_table,
                  g_mix, w_in, conv_w, cmp_w, w_nsa_out, w_o, moe_w, g_fin):
    b, t, _ = x.shape
    past = page_table.shape[1] * PAGE_SIZE
    x2d = x.reshape(b * t, D_MODEL)
    gm = g_mix.reshape(1, D_MODEL)
    a, q, qr, kc_r, vc_r, ks, vs, kw, vw, ng, kvb = _proj_in(x2d, gm, w_in, past + jnp.arange(t), tm=256)
    a3 = a.reshape(b, t, C_CONV)
    hist = jnp.pad(s_conv, ((0, 0), (_HALO - (CONV_K - 1), 0), (0, 0)))
    y_conv = _conv(a3, hist, *conv_w, bb=16, tt=t)
    kc, vc = _compress_sample(_to_feature_major(ck_cmp), _to_feature_major(cv_cmp), page_table, cmp_w)
    sk_t, sv_t = _to_feature_major(sk_win), _to_feature_major(sv_win)
    o = _nsa_sample(q, qr, ng, kc, vc, kvb, sk_t, sv_t, _to_feature_major(ck_sel), _to_feature_major(cv_sel),
                    page_table, tq=t)
    h = _merge(x2d, gm, y_conv.reshape(b * t, D_MODEL), o.reshape(b * t, Q_W), w_in, w_nsa_out, w_o, tm=256)
    y = _moe(h, *moe_w, g_fin, tm=256)
    heads = lambda z: z.reshape(b, t, N_KV, HEAD_DIM)
    k_win = _from_feature_major(jnp.concatenate([sk_t, _to_feature_major(heads(kw))], axis=2)[:, :, t:])
    v_win = _from_feature_major(jnp.concatenate([sv_t, _to_feature_major(heads(vw))], axis=2)[:, :, t:])
    conv_state = jnp.concatenate([s_conv, a3], axis=1)[:, -(CONV_K - 1):]
    state = (heads(kc_r), heads(vc_r), heads(ks), heads(vs), k_win, v_win, conv_state)
    return y.reshape(b, t, D_MODEL), state


def kernel(x_prompt, x_sample, cache_k_cmp, cache_v_cmp, cache_k_sel, cache_v_sel, state_k_win, state_v_win,
           state_conv, page_table, g_mix, w_in, w_dw, b_dw, conv_norm_g, conv_norm_b, w_conv_out,
           w_ck1, b_ck1, w_ck2, w_cv1, b_cv1, w_cv2, w_nsa_out, w_o, g_ffn, w_rg, b_rg, w_re, b_re,
           w_gate, w_up, w_down, g_final):
    depth = g_mix.shape[0]
    assert depth == 1, "single-layer trunk"
    l = 0
    conv_w = (w_dw[l], b_dw[l], conv_norm_g[l], conv_norm_b[l], w_conv_out[l])
    cmp_w = (w_ck1[l], b_ck1[l], w_ck2[l], w_cv1[l], b_cv1[l], w_cv2[l])
    moe_w = (g_ffn[l].reshape(1, D_MODEL), w_rg[l], b_rg[l], w_re[l], b_re[l], w_gate[l], w_up[l], w_down[l])
    g_fin = g_final.reshape(1, D_MODEL)
    y_p, st_p = _layer_prompt(x_prompt, l, g_mix[l], w_in[l], conv_w, cmp_w, w_nsa_out[l], w_o[l], moe_w, g_fin)
    y_s, st_s = _layer_sample(x_sample, cache_k_cmp[l], cache_v_cmp[l], cache_k_sel[l], cache_v_sel[l],
                              state_k_win[l], state_v_win[l], state_conv[l], page_table,
                              g_mix[l], w_in[l], conv_w, cmp_w, w_nsa_out[l], w_o[l], moe_w, g_fin)
    stack = lambda z: z[None]
    return (y_p, y_s) + tuple(stack(z) for z in st_p) + tuple(stack(z) for z in st_s)
```

```python
import functools

import numpy as np
import jax
import jax.numpy as jnp
from jax import lax
from jax.experimental import pallas as pl
from jax.experimental.pallas import tpu as pltpu

D_MODEL = 1024
C_CONV = 512
CONV_K = 31
N_HEADS = 8
N_KV = 2
HEAD_DIM = 64
HPG = N_HEADS // N_KV
CMP_BLOCK = 32
CMP_STRIDE = 16
SEL_BLOCK = 64
N_SEL = 16
WINDOW = 512
PAGE_SIZE = 128
ROPE_THETA = 10000.0
FORCE_BONUS = 1000.0
NEG_INF = -1e30
N_GROUPS = 4
EXP_PER_GROUP = 8
N_EXPERTS = N_GROUPS * EXP_PER_GROUP
D_EXPERT = 128
EPS = 1e-6
KV_W = N_KV * HEAD_DIM
Q_W = N_HEADS * HEAD_DIM
SCALE = HEAD_DIM ** -0.5
HALF = HEAD_DIM // 2
SCALE_LOG2E = SCALE * 1.4426950408889634

LANES = 128
SUBLANES = 8
VMEM_LIMIT = 56 * 1024 * 1024

BF16 = jnp.bfloat16
F32 = jnp.float32

_OFF_U = 0
_OFF_Q = 2 * C_CONV
_OFF_KV = _OFF_Q + Q_W
_OFF_NG = _OFF_KV + 6 * KV_W
_OFF_MG = _OFF_NG + 3 * N_HEADS
_IN_COLS = _OFF_MG + 2 * D_MODEL
_NG_PAD = 32


def _cparams(sem):
    return pltpu.CompilerParams(dimension_semantics=sem, vmem_limit_bytes=VMEM_LIMIT)


def _dot(a, b):
    return jnp.dot(a, b, preferred_element_type=F32)


def _dot_nt(a, b):
    return lax.dot_general(a, b, (((1,), (1,)), ((), ())), preferred_element_type=F32)


def _rms(x, g):
    ms = jnp.mean(x * x, axis=-1, keepdims=True)
    return x * lax.rsqrt(ms + EPS) * g


def _split3(x):
    hi = x.astype(BF16)
    r1 = x - hi.astype(F32)
    mid = r1.astype(BF16)
    lo = (r1 - mid.astype(F32)).astype(BF16)
    return hi, mid, lo


def _rope_tables(pos):
    inv_freq = ROPE_THETA ** (-jnp.arange(HALF, dtype=F32) * 2.0 / HEAD_DIM)
    ang = pos.astype(F32)[:, None] * inv_freq[None, :]
    return jnp.cos(ang), jnp.sin(ang)


def _rope128(x, cos, sin_signed, first_half):
    swapped = jnp.where(first_half, pltpu.roll(x, LANES - HALF, 1), pltpu.roll(x, HALF, 1))
    return x * cos + swapped * sin_signed


def _rope_t(xh, cos, sin):
    x1, x2 = xh[:HALF], xh[HALF:]
    return jnp.concatenate([x1 * cos - x2 * sin, x2 * cos + x1 * sin], axis=0)


def _proj_in_body(x_ref, g_ref, wu_ref, wq_ref, wkv_ref, wng_ref, cos_ref, sin_ref,
                  a_ref, q_ref, qr_ref, kc_ref, vc_ref, ks_ref, vs_ref, kw_ref, vw_ref, ng_ref, kvb_ref):
    xn = _rms(x_ref[...], g_ref[...]).astype(BF16)
    cos = cos_ref[...]
    sin = sin_ref[...]
    lane = lax.broadcasted_iota(jnp.int32, cos.shape, 1)
    first_half = (lane % HEAD_DIM) < HALF

    u = _dot(xn, wu_ref[...])
    a_ref[...] = u[:, :C_CONV] * jax.nn.sigmoid(u[:, C_CONV:])

    q = _dot(xn, wq_ref[...])
    q_ref[...] = (q * SCALE).astype(BF16)
    for c in range(Q_W // LANES):
        qc = q[:, c * LANES:(c + 1) * LANES]
        qr_ref[:, c * LANES:(c + 1) * LANES] = (_rope128(qc, cos, sin, first_half) * SCALE).astype(BF16)

    kv = _dot(xn, wkv_ref[...])
    kc = kv[:, 0 * KV_W:1 * KV_W]
    vc = kv[:, 1 * KV_W:2 * KV_W]
    ks = _rope128(kv[:, 2 * KV_W:3 * KV_W], cos, sin, first_half)
    vs = kv[:, 3 * KV_W:4 * KV_W]
    kw = _rope128(kv[:, 4 * KV_W:5 * KV_W], cos, sin, first_half)
    vw = kv[:, 5 * KV_W:6 * KV_W]
    kc_ref[...] = kc
    vc_ref[...] = vc
    ks_ref[...] = ks
    vs_ref[...] = vs
    kw_ref[...] = kw
    vw_ref[...] = vw
    for i, piece in enumerate((kc, vc, ks, vs, kw, vw)):
        kvb_ref[:, i * KV_W:(i + 1) * KV_W] = piece.astype(BF16)

    ng_ref[...] = jax.nn.sigmoid(_dot(xn, wng_ref[...]))


def _proj_in(x2d, g_mix, w_in, pos_period, tm):
    n = x2d.shape[0]
    assert n % tm == 0
    p = pos_period.shape[0]
    if p < tm:
        assert tm % p == 0
        pos_period = jnp.tile(pos_period, tm // p)
        p = tm
    assert p % tm == 0
    cos, sin = _rope_tables(pos_period)
    cos = jnp.tile(cos, (1, LANES // HALF))
    sin = jnp.tile(jnp.concatenate([-sin, sin], axis=1), (1, LANES // HEAD_DIM))

    wb = w_in.astype(BF16)
    wu = wb[:, _OFF_U:_OFF_Q]
    wq = wb[:, _OFF_Q:_OFF_KV]
    wkv = wb[:, _OFF_KV:_OFF_NG]
    wng = jnp.pad(wb[:, _OFF_NG:_OFF_MG], ((0, 0), (0, LANES - 3 * N_HEADS)))

    nper = p // tm
    row = lambda w: pl.BlockSpec((tm, w), lambda i: (i, 0))
    full = lambda a: pl.BlockSpec(a.shape, lambda i: (0, 0))
    tab = pl.BlockSpec((tm, LANES), lambda i: (i % nper, 0))
    outs = [
        jax.ShapeDtypeStruct((n, C_CONV), F32),
        jax.ShapeDtypeStruct((n, Q_W), BF16),
        jax.ShapeDtypeStruct((n, Q_W), BF16),
    ] + [jax.ShapeDtypeStruct((n, KV_W), F32)] * 6 + [
        jax.ShapeDtypeStruct((n, LANES), F32),
        jax.ShapeDtypeStruct((n, 6 * KV_W), BF16),
    ]
    out_specs = [row(C_CONV), row(Q_W), row(Q_W)] + [row(KV_W)] * 6 + [row(LANES), row(6 * KV_W)]
    return pl.pallas_call(
        _proj_in_body,
        grid=(n // tm,),
        in_specs=[row(D_MODEL), full(g_mix), full(wu), full(wq), full(wkv), full(wng), tab, tab],
        out_specs=out_specs,
        out_shape=outs,
        compiler_params=_cparams(("parallel",)),
        name="proj_in",
    )(x2d, g_mix, wu, wq, wkv, wng, cos, sin)


_ZT_ROWS = Q_W + 6 * KV_W + _NG_PAD


def _proj_in_t_body(x_ref, g_ref, wu_ref, wt_ref, cos_ref, sin_ref,
                    a_ref, qt_ref, qrt_ref, kct_ref, vct_ref, kst_ref, vst_ref, kwt_ref, vwt_ref, ngt_ref,
                    kc_ref, vc_ref, ks_ref, kw_ref, vstb_ref, vwtb_ref, *, tm, nt, nsw):
    xn = _rms(x_ref[...], g_ref[...]).astype(BF16)
    cos = cos_ref[...]
    sin = sin_ref[...]

    u = _dot(xn, wu_ref[...])
    a_ref[...] = u[:, :C_CONV] * jax.nn.sigmoid(u[:, C_CONV:])

    z = _dot_nt(wt_ref[...], xn)
    for h in range(N_HEADS):
        qh = z[h * HEAD_DIM:(h + 1) * HEAD_DIM]
        qt_ref[0, h * HEAD_DIM:(h + 1) * HEAD_DIM, :] = (qh * SCALE_LOG2E).astype(BF16)
        qrt_ref[0, h * HEAD_DIM:(h + 1) * HEAD_DIM, :] = (_rope_t(qh, cos, sin) * SCALE_LOG2E).astype(BF16)

    def kv_rows(i):
        return z[Q_W + i * KV_W:Q_W + (i + 1) * KV_W]

    def rope_groups(x):
        return jnp.concatenate([_rope_t(x[g * HEAD_DIM:(g + 1) * HEAD_DIM], cos, sin) for g in range(N_KV)], axis=0)

    kct, vct, vst, vwt = kv_rows(0), kv_rows(1), kv_rows(3), kv_rows(5)
    kst = rope_groups(kv_rows(2))
    kwt = rope_groups(kv_rows(4))
    kct_ref[0] = kct
    vct_ref[0] = vct
    kst_ref[0] = kst
    vst_ref[0] = vst
    kwt_ref[0] = kwt
    vwt_ref[0] = vwt
    kc_ref[...] = kct.T.astype(BF16)
    vc_ref[...] = vct.T.astype(BF16)
    tok = (pl.program_id(0) % nt) * tm + lax.broadcasted_iota(jnp.int32, (tm, nsw), 0)
    onehot = jnp.where(tok // SEL_BLOCK == lax.broadcasted_iota(jnp.int32, (tm, nsw), 1), 1.0, 0.0).astype(BF16)
    ks_rm = kst.T.astype(BF16)
    for g in range(N_KV):
        ks_ref[:, g * (HEAD_DIM + nsw):g * (HEAD_DIM + nsw) + HEAD_DIM] = ks_rm[:, g * HEAD_DIM:(g + 1) * HEAD_DIM]
        ks_ref[:, g * (HEAD_DIM + nsw) + HEAD_DIM:(g + 1) * (HEAD_DIM + nsw)] = onehot
    kw_ref[...] = kwt.T.astype(BF16)
    for j in range(tm // LANES):
        vstb_ref[0, j] = vst[:, j * LANES:(j + 1) * LANES].astype(BF16)
        vwtb_ref[0, j] = vwt[:, j * LANES:(j + 1) * LANES].astype(BF16)

    ngt_ref[0] = jax.nn.sigmoid(z[Q_W + 6 * KV_W:])


def _proj_in_t(x, g_mix, w_in, tm):
    b, t, _ = x.shape
    assert t % tm == 0 and tm % LANES == 0
    n = b * t
    nt = t // tm
    nsw = _sel_width(t)
    ks_w = N_KV * (HEAD_DIM + nsw)
    cos, sin = _rope_tables(jnp.arange(t))
    cos_t, sin_t = cos.T, sin.T
    wu = w_in[:, _OFF_U:_OFF_Q].astype(BF16)
    wt = jnp.pad(w_in.T[_OFF_Q:_OFF_MG], ((0, _NG_PAD - 3 * N_HEADS), (0, 0))).astype(BF16)

    row = lambda w: pl.BlockSpec((tm, w), lambda i: (i, 0))
    full = lambda a: pl.BlockSpec(a.shape, lambda i: (0, 0))
    tab = pl.BlockSpec((HALF, tm), lambda i: (0, i % nt))
    feat = lambda r: pl.BlockSpec((1, r, tm), lambda i: (i // nt, 0, i % nt))
    chunks = pl.BlockSpec((1, tm // LANES, KV_W, LANES), lambda i: (i // nt, i % nt, 0, 0))
    outs = (
        [jax.ShapeDtypeStruct((n, C_CONV), F32),
         jax.ShapeDtypeStruct((b, Q_W, t), BF16), jax.ShapeDtypeStruct((b, Q_W, t), BF16)]
        + [jax.ShapeDtypeStruct((b, KV_W, t), F32)] * 6
        + [jax.ShapeDtypeStruct((b, _NG_PAD, t), F32)]
        + [jax.ShapeDtypeStruct((n, KV_W), BF16)] * 2
        + [jax.ShapeDtypeStruct((n, ks_w), BF16), jax.ShapeDtypeStruct((n, KV_W), BF16)]
        + [jax.ShapeDtypeStruct((b, t // LANES, KV_W, LANES), BF16)] * 2
    )
    out_specs = ([row(C_CONV), feat(Q_W), feat(Q_W)] + [feat(KV_W)] * 6 + [feat(_NG_PAD)]
                 + [row(KV_W), row(KV_W), row(ks_w), row(KV_W)] + [chunks] * 2)
    return pl.pallas_call(
        functools.partial(_proj_in_t_body, tm=tm, nt=nt, nsw=nsw),
        grid=(n // tm,),
        in_specs=[row(D_MODEL), full(g_mix), full(wu), full(wt), tab, tab],
        out_specs=out_specs,
        out_shape=outs,
        compiler_params=_cparams(("parallel",)),
        name="proj_in_t",
    )(x.reshape(n, D_MODEL), g_mix, wu, wt, cos_t, sin_t)


_HALO = 32


def _conv_body(a_ref, halo_ref, wdw_ref, bdw_ref, ng_ref, nb_ref, wout_ref, y_ref, full_ref, c_ref, sh_ref,
               *, bb, tt, halo_from_prev_tile):
    if halo_from_prev_tile:
        first = pl.program_id(1) == 0

        @pl.when(first)
        def _():
            full_ref[:, 0:_HALO, :] = jnp.zeros((bb, _HALO, C_CONV), F32)

        @pl.when(jnp.logical_not(first))
        def _():
            full_ref[:, 0:_HALO, :] = halo_ref[...]
    else:
        full_ref[:, 0:_HALO, :] = halo_ref[...]
    full_ref[:, _HALO:_HALO + tt, :] = a_ref[...]

    base = _HALO - (CONV_K - 1)
    rc = min(tt, 64)
    offs = [[o for o in range(base, base + CONV_K) if o % SUBLANES == r] for r in range(SUBLANES)]
    for s in range(bb):
        for r in range(SUBLANES):
            span = offs[r][-1] - r + tt
            sh_ref[r, 0:span, :] = full_ref[s, r:r + span, :]
        for r0 in range(0, tt, rc):
            for l0 in range(0, C_CONV, LANES):
                acc = jnp.zeros((rc, LANES), F32)
                for r in range(SUBLANES):
                    for o in offs[r]:
                        k = o - base
                        rows = slice(r0 + o - r, r0 + o - r + rc)
                        acc = acc + sh_ref[r, rows, l0:l0 + LANES] * wdw_ref[k:k + 1, l0:l0 + LANES]
                c_ref[s * tt + r0:s * tt + r0 + rc, l0:l0 + LANES] = acc + bdw_ref[:, l0:l0 + LANES]

    c = c_ref[...]
    mu = jnp.mean(c, axis=-1, keepdims=True)
    d = c - mu
    var = jnp.mean(d * d, axis=-1, keepdims=True)
    cn = d * lax.rsqrt(var + EPS) * ng_ref[...] + nb_ref[...]
    act = (cn * jax.nn.sigmoid(cn)).astype(BF16)
    y_ref[...] = _dot(act, wout_ref[...]).reshape(bb, tt, D_MODEL)


def _conv(a3, hist, w_dw, b_dw, n_g, n_b, w_out, bb, tt):
    b, t, _ = a3.shape
    assert t % tt == 0 and b % bb == 0 and tt % SUBLANES == 0
    wdw = jnp.pad(w_dw, ((0, 1), (0, 0)))
    wout = w_out.astype(BF16)
    from_prev = hist is None
    if from_prev:
        assert tt % _HALO == 0
        ratio = tt // _HALO
        halo_arr = a3
        halo_spec = pl.BlockSpec((bb, _HALO, C_CONV), lambda bi, i: (bi, jnp.maximum(i * ratio - 1, 0), 0))
    else:
        assert t == tt
        halo_arr = hist
        halo_spec = pl.BlockSpec((bb, _HALO, C_CONV), lambda bi, i: (bi, 0, 0))
    vec = pl.BlockSpec((1, C_CONV), lambda bi, i: (0, 0))
    return pl.pallas_call(
        functools.partial(_conv_body, bb=bb, tt=tt, halo_from_prev_tile=from_prev),
        grid=(b // bb, t // tt),
        in_specs=[
            pl.BlockSpec((bb, tt, C_CONV), lambda bi, i: (bi, i, 0)),
            halo_spec,
            pl.BlockSpec(wdw.shape, lambda bi, i: (0, 0)),
            vec, vec, vec,
            pl.BlockSpec(wout.shape, lambda bi, i: (0, 0)),
        ],
        out_specs=pl.BlockSpec((bb, tt, D_MODEL), lambda bi, i: (bi, i, 0)),
        out_shape=jax.ShapeDtypeStruct((b, t, D_MODEL), F32),
        scratch_shapes=[pltpu.VMEM((bb, _HALO + tt, C_CONV), F32), pltpu.VMEM((bb * tt, C_CONV), F32),
                        pltpu.VMEM((SUBLANES, _HALO + tt, C_CONV), F32)],
        compiler_params=_cparams(("parallel", "arbitrary")),
        name="conv",
    )(a3, halo_arr, wdw, b_dw.reshape(1, -1), n_g.reshape(1, -1), n_b.reshape(1, -1), wout)


def _compress_weights(w1, b1, w2):
    eye = jnp.eye(N_KV, dtype=F32)
    big = jnp.einsum("gh,lde->lgdhe", eye, w1).reshape(CMP_BLOCK, KV_W, KV_W)
    wa = big[:CMP_STRIDE].reshape(CMP_STRIDE * KV_W, KV_W)
    wb = big[CMP_STRIDE:].reshape(CMP_STRIDE * KV_W, KV_W)
    wcat = jnp.concatenate([wa, wb], axis=1).astype(BF16)
    w2bd = jnp.einsum("gh,ef->gehf", eye, w2).reshape(KV_W, KV_W)
    b1t = jnp.tile(b1, N_KV).reshape(1, KV_W)
    return wcat, b1t, w2bd.astype(BF16), w2bd.T.astype(BF16)


def _compress_hidden(p, b1_ref):
    s = p.shape[0]
    pre = p[:, :KV_W] + pltpu.roll(p[:, KV_W:], s - 1, 0) + b1_ref[...]
    return (pre * jax.nn.sigmoid(pre)).astype(BF16)


def _compress_prompt_body(k_ref, v_ref, wk_ref, bk_ref, wk2_ref, wv_ref, bv_ref, wv2t_ref, kc_ref, vct_ref):
    hid_k = _compress_hidden(_dot(k_ref[0], wk_ref[...]), bk_ref)
    kc_ref[0] = _dot(hid_k, wk2_ref[...]).astype(BF16)
    hid_v = _compress_hidden(_dot(v_ref[0], wv_ref[...]), bv_ref)
    vct_ref[0] = _dot_nt(wv2t_ref[...], hid_v).astype(BF16)


def _compress_prompt(kc_rm, vc_rm, b, t, cmp_w):
    s = t // CMP_STRIDE
    feat = CMP_STRIDE * KV_W
    k_rows = kc_rm.reshape(b, s, feat)
    v_rows = vc_rm.reshape(b, s, feat)
    wk, bk, wk2, _ = _compress_weights(*cmp_w[:3])
    wv, bv, _, wv2t = _compress_weights(*cmp_w[3:])
    rows = pl.BlockSpec((1, s, feat), lambda i: (i, 0, 0))
    full = lambda a: pl.BlockSpec(a.shape, lambda i: (0, 0))
    return pl.pallas_call(
        _compress_prompt_body,
        grid=(b,),
        in_specs=[rows, rows, full(wk), full(bk), full(wk2), full(wv), full(bv), full(wv2t)],
        out_specs=[pl.BlockSpec((1, s, KV_W), lambda i: (i, 0, 0)), pl.BlockSpec((1, KV_W, s), lambda i: (i, 0, 0))],
        out_shape=[jax.ShapeDtypeStruct((b, s, KV_W), BF16), jax.ShapeDtypeStruct((b, KV_W, s), BF16)],
        compiler_params=_cparams(("parallel",)),
        name="compress_prompt",
    )(k_rows, v_rows, wk, bk, wk2, wv, bv, wv2t)


def _gather_start(pools, bufs, sems, pt_ref, seq, slot, n_pages, dst_of):
    def body(j, carry):
        page = pt_ref[seq * n_pages + j]
        for pool_ref, buf_ref, sem_ref in zip(pools, bufs, sems):
            pltpu.make_async_copy(pool_ref.at[page], dst_of(buf_ref, slot, j), sem_ref.at[slot]).start()
        return carry
    lax.fori_loop(0, n_pages, body, 0, unroll=8)


def _gather_wait(bufs, sems, slot):
    for buf_ref, sem_ref in zip(bufs, sems):
        pltpu.make_async_copy(buf_ref.at[slot], buf_ref.at[slot], sem_ref.at[slot]).wait()


def _paged_prologue(pools, bufs, sems, pt_ref, n_pages, dst_of):
    i = pl.program_id(0)
    n = pl.num_programs(0)
    slot = i % 2

    @pl.when(i == 0)
    def _():
        _gather_start(pools, bufs, sems, pt_ref, i, slot, n_pages, dst_of)

    @pl.when(i + 1 < n)
    def _():
        _gather_start(pools, bufs, sems, pt_ref, i + 1, 1 - slot, n_pages, dst_of)

    _gather_wait(bufs, sems, slot)
    return slot


def _dst_page_major(buf_ref, slot, j):
    return buf_ref.at[slot, j]


def _dst_token_lanes(buf_ref, slot, j):
    return buf_ref.at[slot, :, pl.ds(pl.multiple_of(j * PAGE_SIZE, PAGE_SIZE), PAGE_SIZE)]


def _compress_sample_body(pt_ref, kpool_ref, vpool_ref, perm_ref, wk_ref, bk_ref, wk2_ref, wv_ref, bv_ref, wv2_ref,
                          kc_ref, vc_ref, kbuf, vbuf, ksem, vsem, tok_ref, *, n_pages):
    slot = _paged_prologue((kpool_ref, vpool_ref), (kbuf, vbuf), (ksem, vsem), pt_ref, n_pages, _dst_page_major)
    spp = PAGE_SIZE // CMP_STRIDE
    n_rows = n_pages * spp

    def compress(buf, w_ref, b_ref, w2_ref, out_ref):
        def tr(j, carry):
            tok_ref[j] = _dot_nt(perm_ref[...], buf[slot, j].astype(BF16))
            return carry
        lax.fori_loop(0, n_pages, tr, 0, unroll=32)
        p = jnp.zeros((n_rows, 2 * KV_W), F32)
        for l in range(0, CMP_STRIDE, 2):
            y = jnp.concatenate(
                [tok_ref[:, (l + u) * spp:(l + u + 1) * spp, :].reshape(n_rows, KV_W) for u in range(2)], axis=1)
            p = p + _dot(y.astype(BF16), w_ref[l * KV_W:(l + 2) * KV_W, :])
        out_ref[0] = _dot(_compress_hidden(p, b_ref), w2_ref[...]).astype(BF16)

    compress(kbuf, wk_ref, bk_ref, wk2_ref, kc_ref)
    compress(vbuf, wv_ref, bv_ref, wv2_ref, vc_ref)


def _compress_sample(pool_k_t, pool_v_t, page_table, cmp_w):
    b, n_pages = page_table.shape
    s = n_pages * PAGE_SIZE // CMP_STRIDE
    wk, bk, wk2, _ = _compress_weights(*cmp_w[:3])
    wv, bv, wv2, _ = _compress_weights(*cmp_w[3:])
    spp = PAGE_SIZE // CMP_STRIDE
    r = np.arange(PAGE_SIZE)
    perm = jnp.asarray(((r % spp) * CMP_STRIDE + r // spp)[:, None] == r[None, :], dtype=BF16)
    full = lambda a: pl.BlockSpec(a.shape, lambda i, pt: (0, 0))
    out = pl.BlockSpec((1, s, KV_W), lambda i, pt: (i, 0, 0))
    anyspec = pl.BlockSpec(memory_space=pl.ANY)
    grid_spec = pltpu.PrefetchScalarGridSpec(
        num_scalar_prefetch=1,
        grid=(b,),
        in_specs=[anyspec, anyspec, full(perm), full(wk), full(bk), full(wk2), full(wv), full(bv), full(wv2)],
        out_specs=[out, out],
        scratch_shapes=[
            pltpu.VMEM((2, n_pages, KV_W, PAGE_SIZE), F32),
            pltpu.VMEM((2, n_pages, KV_W, PAGE_SIZE), F32),
            pltpu.SemaphoreType.DMA((2,)),
            pltpu.SemaphoreType.DMA((2,)),
            pltpu.VMEM((n_pages, PAGE_SIZE, KV_W), F32),
        ],
    )
    return pl.pallas_call(
        functools.partial(_compress_sample_body, n_pages=n_pages),
        grid_spec=grid_spec,
        out_shape=[jax.ShapeDtypeStruct((b, s, KV_W), BF16)] * 2,
        compiler_params=_cparams(("arbitrary",)),
        name="compress_sample",
    )(page_table.reshape(-1), pool_k_t, pool_v_t, perm, wk, bk, wk2, wv, bv, wv2)


def _overlap_np(nc_pad, ns_pad):
    i = np.arange(nc_pad)[:, None]
    j = np.arange(ns_pad)[None, :]
    return ((i * CMP_STRIDE < (j + 1) * SEL_BLOCK) & (i * CMP_STRIDE + CMP_BLOCK > j * SEL_BLOCK)).astype(np.float32)


def _importance_bonus(imp, j, t):
    cur = t // SEL_BLOCK
    forced = (j == 0) | (j == cur) | (j == cur - 1)
    blk_ok = j * SEL_BLOCK <= t
    return jnp.where(blk_ok, imp + jnp.where(forced, FORCE_BONUS, 0.0), NEG_INF)


def _sel_width(t):
    return -(-max(t // SEL_BLOCK, 1) // HEAD_DIM) * HEAD_DIM


def _softmax2_cols(s):
    m = jnp.max(s, axis=0, keepdims=True)
    e = jnp.exp2(s - m)
    return e / jnp.sum(e, axis=0, keepdims=True)


def _top_k_cols(imp, k):
    ns = imp.shape[0]
    j = lax.broadcasted_iota(jnp.int32, imp.shape, 0)
    cnt = jnp.zeros(imp.shape, F32)
    for i in range(ns):
        row = imp[i:i + 1, :]
        tie = jnp.where(j > i, 1.0, 0.0)
        cnt = cnt + jnp.where(row > imp, 1.0, jnp.where(row == imp, tie, 0.0))
    return jnp.where(cnt < k, 1.0, 0.0)


def _heads_on_lanes(xt, g, rows_per_head):
    return jnp.concatenate(
        [xt[(g * HPG + h) * rows_per_head:(g * HPG + h + 1) * rows_per_head] for h in range(HPG)], axis=1)


def _gate_row(ngt, g, branch):
    return jnp.concatenate([ngt[(g * HPG + h) * 3 + branch:(g * HPG + h) * 3 + branch + 1] for h in range(HPG)], axis=1)


def _nsa_prompt_body(qt_ref, qrt_ref, ngt_ref, kc_ref, vct_ref, ks_ref, kw_ref, vst_ref, vwt_ref, ovt_ref,
                     o_ref, *, qb, kc_len, nsw):
    i = pl.program_id(1)
    start = i * qb
    t_row = start + lax.broadcasted_iota(jnp.int32, (1, qb), 1)
    t4 = jnp.concatenate([t_row] * HPG, axis=1)
    qt = qt_ref[0]
    qrt = qrt_ref[0]
    ngt = ngt_ref[0]
    nc = kc_ref.shape[1]
    ns = ovt_ref.shape[0]
    n_chunks = (start + qb + kc_len - 1) // kc_len
    sub = kc_len // LANES
    w_len = WINDOW + qb
    w_start = pl.multiple_of(jnp.maximum(start - WINDOW, 0), qb)
    w_blk = w_start // LANES

    groups = range(N_KV)
    glanes = [slice(g * HEAD_DIM, (g + 1) * HEAD_DIM) for g in groups]
    qr = [_heads_on_lanes(qrt, g, HEAD_DIM) for g in groups]
    kw_aug = HEAD_DIM + nsw
    j_blk = lax.broadcasted_iota(jnp.int32, (ns, qb), 0)
    c_end = lax.broadcasted_iota(jnp.int32, (nc, 1), 0) * CMP_STRIDE + (CMP_BLOCK - 1)
    ovt = ovt_ref[...]
    c_bias = jnp.concatenate([jnp.where(c_end <= t_row, 0.0, NEG_INF)] * HPG, axis=1)

    o_c, rhs = [], []
    for g in groups:
        s_c = _dot(kc_ref[0][:, glanes[g]], _heads_on_lanes(qt, g, HEAD_DIM))
        p_c = _softmax2_cols(s_c + c_bias)
        p_c = p_c * jnp.where(t4 >= CMP_BLOCK - 1, 1.0, 0.0)
        o_c.append(_dot(vct_ref[0][glanes[g], :], p_c.astype(BF16)))
        p_sum = p_c[:, 0:qb] + p_c[:, qb:2 * qb] + p_c[:, 2 * qb:3 * qb] + p_c[:, 3 * qb:4 * qb]
        hi, mid, lo = _split3(p_sum)
        imp = _dot(ovt, hi) + _dot(ovt, mid) + _dot(ovt, lo)
        sel = _top_k_cols(_importance_bonus(imp, j_blk, t_row), min(N_SEL, ns))
        open_blk = (sel > 0.5) & (j_blk * SEL_BLOCK <= t_row)
        blk_bias = jnp.where(open_blk, 0.0, NEG_INF)
        if nsw > ns:
            blk_bias = jnp.concatenate([blk_bias, jnp.zeros((nsw - ns, qb), F32)], axis=0)
        rhs.append(jnp.concatenate([qr[g], jnp.concatenate([blk_bias.astype(BF16)] * HPG, axis=1)], axis=0))

    def chunk(c, carry, causal):
        k0 = pl.multiple_of(c * kc_len, kc_len)
        vblk = vst_ref[0, pl.ds(c * sub, sub)]
        if causal:
            key = k0 + lax.broadcasted_iota(jnp.int32, (kc_len, 1), 0)
            causal_bias = jnp.concatenate([jnp.where(key <= t_row, 0.0, NEG_INF)] * HPG, axis=1)
        out = []
        for g in groups:
            m, l, acc = carry[g]
            kch = ks_ref[0, pl.ds(k0, kc_len), g * kw_aug:(g + 1) * kw_aug]
            vch = jnp.concatenate([vblk[u][glanes[g], :] for u in range(sub)], axis=1)
            s = _dot(kch, rhs[g])
            if causal:
                s = s + causal_bias
            m_new = jnp.maximum(m, jnp.max(s, axis=0, keepdims=True))
            alpha = jnp.exp2(m - m_new)
            p = jnp.exp2(s - m_new)
            l = alpha * l + jnp.sum(p, axis=0, keepdims=True)
            acc = alpha * acc + _dot(vch, p.astype(BF16))
            out.append((m_new, l, acc))
        return tuple(out)

    init = tuple((jnp.full((1, HPG * qb), NEG_INF, F32), jnp.zeros((1, HPG * qb), F32),
                  jnp.zeros((HEAD_DIM, HPG * qb), F32)) for _ in groups)
    carry = lax.fori_loop(0, n_chunks - 1, functools.partial(chunk, causal=False), init)
    carry = chunk(n_chunks - 1, carry, causal=True)

    wblk = vwt_ref[0, pl.ds(w_blk, w_len // LANES)]
    key = w_start + lax.broadcasted_iota(jnp.int32, (w_len, 1), 0)
    w_bias = jnp.concatenate([jnp.where((key <= t_row) & (t_row - key < WINDOW), 0.0, NEG_INF)] * HPG, axis=1)
    o_heads = []
    for g in groups:
        _, l, acc = carry[g]
        kwc = kw_ref[0, pl.ds(w_start, w_len), glanes[g]]
        vwc = jnp.concatenate([wblk[u][glanes[g], :] for u in range(w_len // LANES)], axis=1)
        p_w = _softmax2_cols(_dot(kwc, qr[g]) + w_bias)
        o_w = _dot(vwc, p_w.astype(BF16))
        o = _gate_row(ngt, g, 0) * o_c[g] + _gate_row(ngt, g, 1) * (acc / l) + _gate_row(ngt, g, 2) * o_w
        o_heads += [o[:, h * qb:(h + 1) * qb] for h in range(HPG)]
    o_ref[0] = jnp.concatenate(o_heads, axis=0).T.astype(o_ref.dtype)


def _nsa_prompt(qt, qrt, ngt, kc, vct, ks_aug, kw_rm, vstb, vwtb, b, t, qb, kc_len):
    assert t % kc_len == 0 and kc_len % qb == 0 and t >= WINDOW + qb and WINDOW % qb == 0 and qb % LANES == 0
    ns = t // SEL_BLOCK
    nsw = _sel_width(t)
    ks_w = N_KV * (HEAD_DIM + nsw)
    nc_pad = t // CMP_STRIDE
    ovt = jnp.asarray(_overlap_np(nc_pad, ns).T, dtype=BF16)
    tile = lambda r: pl.BlockSpec((1, r, qb), lambda bi, i: (bi, 0, i))
    per_b = lambda shp: pl.BlockSpec((1,) + shp, lambda bi, i: (bi,) + (0,) * len(shp))
    const = lambda a: pl.BlockSpec(a.shape, lambda bi, i: (0,) * a.ndim)
    return pl.pallas_call(
        functools.partial(_nsa_prompt_body, qb=qb, kc_len=kc_len, nsw=nsw),
        grid=(b, t // qb),
        in_specs=[
            tile(Q_W), tile(Q_W), tile(_NG_PAD),
            per_b((nc_pad, KV_W)), per_b((KV_W, nc_pad)),
            per_b((t, ks_w)), per_b((t, KV_W)),
            per_b((t // LANES, KV_W, LANES)), per_b((t // LANES, KV_W, LANES)),
            const(ovt),
        ],
        out_specs=pl.BlockSpec((1, qb, Q_W), lambda bi, i: (bi, i, 0)),
        out_shape=jax.ShapeDtypeStruct((b, t, Q_W), BF16),
        compiler_params=_cparams(("parallel", "arbitrary")),
        name="nsa_prompt",
    )(qt, qrt, ngt, kc, vct, ks_aug.reshape(b, t, ks_w), kw_rm.reshape(b, t, KV_W), vstb, vwtb, ovt)


def _block_diag_q(q, tq):
    z = jnp.zeros((HPG * tq, HEAD_DIM), q.dtype)
    rows = []
    for g in range(N_KV):
        qs = jnp.concatenate([q[:, (g * HPG + h) * HEAD_DIM:(g * HPG + h + 1) * HEAD_DIM] for h in range(HPG)], axis=0)
        rows.append(jnp.concatenate([qs, z] if g == 0 else [z, qs], axis=1))
    return jnp.concatenate(rows, axis=0)


def _rows_ghq(x, tq):
    return jnp.concatenate([x[g * tq:(g + 1) * tq] for g in range(N_KV) for _ in range(HPG)], axis=0)


def _top_k_rows(imp, imp_t, k, ni):
    r, ns = imp.shape
    ii = lax.broadcasted_iota(jnp.int32, (ni, ns), 0)
    jj = lax.broadcasted_iota(jnp.int32, (ni, ns), 1)
    tie = jnp.where(ii < jj, 1.0, 0.0)
    out = []
    for x in range(r):
        col = imp_t[:ni, x:x + 1]
        row = imp[x:x + 1, :]
        beats = jnp.where(col > row, 1.0, jnp.where(col == row, tie, 0.0))
        out.append(jnp.sum(beats, axis=0, keepdims=True))
    return jnp.where(jnp.concatenate(out, axis=0) < k, 1.0, 0.0)


def _nsa_sample_body(pt_ref, q_ref, qr_ref, ng_ref, kc_ref, vc_ref, kvn_ref, kwin_ref, vwin_ref, kpool_ref, vpool_ref,
                     ov_ref, e_ref, o_ref, kbuf, vbuf, ksem, vsem, kcat_ref, *, n_pages, tq):
    slot = _paged_prologue((kpool_ref, vpool_ref), (kbuf, vbuf), (ksem, vsem), pt_ref, n_pages, _dst_token_lanes)
    past = n_pages * PAGE_SIZE
    ns_past = past // SEL_BLOCK
    nrow = N_KV * HPG * tq
    wb = kwin_ref.shape[2]

    @pl.when(pl.program_id(0) == 0)
    def _():
        kcat_ref[KV_W:, :] = e_ref[...]

    t_q = past + lax.broadcasted_iota(jnp.int32, (tq, 1), 0)
    t16 = jnp.concatenate([t_q] * N_KV, axis=0)
    t_rows = _rows_ghq(t16, tq)
    row_g = lax.broadcasted_iota(jnp.int32, (nrow, 1), 0) // (HPG * tq)

    def own_group(x):
        return jnp.where(row_g == 0, x[:, :HEAD_DIM], x[:, HEAD_DIM:])

    q_bd = _block_diag_q(q_ref[0], tq)
    qr_bd = _block_diag_q(qr_ref[0], tq)
    kvn = kvn_ref[0]
    new = lambda idx: kvn[:, idx * KV_W:(idx + 1) * KV_W]

    nc = kc_ref.shape[1]
    s_c = _dot_nt(q_bd, kc_ref[0])
    c_end = lax.broadcasted_iota(jnp.int32, (1, nc), 1) * CMP_STRIDE + (CMP_BLOCK - 1)
    s_c = jnp.where(c_end <= t_rows, s_c, NEG_INF)
    m_c = jnp.max(s_c, axis=-1, keepdims=True)
    e_c = jnp.exp(s_c - m_c)
    p_c = e_c / jnp.sum(e_c, axis=-1, keepdims=True) * jnp.where(t_rows >= CMP_BLOCK - 1, 1.0, 0.0)
    o_c = own_group(_dot(p_c.astype(BF16), vc_ref[0]))
    p_sum = jnp.concatenate(
        [sum(p_c[(g * HPG + h) * tq:(g * HPG + h + 1) * tq] for h in range(HPG)) for g in range(N_KV)], axis=0)
    hi, mid, lo = _split3(p_sum)
    ov = ov_ref[...]
    imp = _dot(hi, ov) + _dot(mid, ov) + _dot(lo, ov)
    ns_pad = imp.shape[1]
    j_blk = lax.broadcasted_iota(jnp.int32, imp.shape, 1)
    imp = _importance_bonus(imp, j_blk, t16)
    imp_sq = jnp.concatenate([imp, jnp.zeros((ns_pad - N_KV * tq, ns_pad), F32)], axis=0)
    ni = -(-(ns_past + 1) // SUBLANES) * SUBLANES
    sel = _top_k_rows(imp, imp_sq.T, min(N_SEL, ns_past + 1), ni)

    bias = jnp.where(sel[:, :ns_past] > 0.5, 0.0, NEG_INF)
    lhs = jnp.concatenate([qr_bd, _rows_ghq(bias, tq).astype(BF16)], axis=1)
    kcat_ref[:KV_W, :] = kbuf[slot].astype(BF16)
    s_past = _dot(lhs, kcat_ref[...])
    new_sel = _rows_ghq(sel[:, ns_past:ns_past + 1], tq) > 0.5
    key_new = past + lax.broadcasted_iota(jnp.int32, (1, tq), 1)
    causal_new = key_new <= t_rows
    s_new = jnp.where(new_sel & causal_new, _dot_nt(qr_bd, new(2)), NEG_INF)
    m = jnp.maximum(jnp.max(s_past, axis=-1, keepdims=True), jnp.max(s_new, axis=-1, keepdims=True))
    p_past = jnp.exp(s_past - m)
    p_new = jnp.exp(s_new - m)
    l = jnp.sum(p_past, axis=-1, keepdims=True) + jnp.sum(p_new, axis=-1, keepdims=True)
    o_s = own_group(_dot_nt(p_past.astype(BF16), vbuf[slot].astype(BF16)) + _dot(p_new.astype(BF16), new(3))) / l

    pos_p = (past - wb) + lax.broadcasted_iota(jnp.int32, (1, wb), 1)
    ok_p = (pos_p <= t_rows) & (t_rows - pos_p < WINDOW) & (pos_p >= 0)
    sw_p = jnp.where(ok_p, _dot(qr_bd, kwin_ref[0].astype(BF16)), NEG_INF)
    sw_n = jnp.where(causal_new, _dot_nt(qr_bd, new(4)), NEG_INF)
    mw = jnp.maximum(jnp.max(sw_p, axis=-1, keepdims=True), jnp.max(sw_n, axis=-1, keepdims=True))
    pw_p = jnp.exp(sw_p - mw)
    pw_n = jnp.exp(sw_n - mw)
    lw = jnp.sum(pw_p, axis=-1, keepdims=True) + jnp.sum(pw_n, axis=-1, keepdims=True)
    o_w = own_group(_dot_nt(pw_p.astype(BF16), vwin_ref[0].astype(BF16)) + _dot(pw_n.astype(BF16), new(5))) / lw

    ng = ng_ref[0]
    gate = lambda br: jnp.concatenate([ng[:, hh * 3 + br:hh * 3 + br + 1] for hh in range(N_HEADS)], axis=0)
    o = gate(0) * o_c + gate(1) * o_s + gate(2) * o_w
    for hh in range(N_HEADS):
        o_ref[0, :, hh * HEAD_DIM:(hh + 1) * HEAD_DIM] = o[hh * tq:(hh + 1) * tq].astype(o_ref.dtype)


def _nsa_sample(q, qr, ng, kc, vc, kvb, k_win_t, v_win_t, pool_k_t, pool_v_t, page_table, tq):
    b, n_pages = page_table.shape
    past = n_pages * PAGE_SIZE
    assert tq < CMP_STRIDE and tq <= SEL_BLOCK and past % SEL_BLOCK == 0 and tq % SUBLANES == 0
    wb = k_win_t.shape[2]
    ns_past = past // SEL_BLOCK
    ns_pad = (ns_past + 1 + LANES - 1) // LANES * LANES
    nc_pad = past // CMP_STRIDE
    ov = jnp.asarray(_overlap_np(nc_pad, ns_pad), dtype=BF16)
    blk = np.arange(past) // SEL_BLOCK
    e = jnp.asarray((np.arange(ns_past)[:, None] == blk[None, :]).astype(np.float32), dtype=BF16)
    per_b = lambda rows, w: pl.BlockSpec((1, rows, w), lambda i, pt: (i, 0, 0))
    anyspec = pl.BlockSpec(memory_space=pl.ANY)
    grid_spec = pltpu.PrefetchScalarGridSpec(
        num_scalar_prefetch=1,
        grid=(b,),
        in_specs=[
            per_b(tq, Q_W), per_b(tq, Q_W), per_b(tq, LANES),
            per_b(nc_pad, KV_W), per_b(nc_pad, KV_W), per_b(tq, 6 * KV_W),
            per_b(KV_W, wb), per_b(KV_W, wb),
            anyspec, anyspec,
            pl.BlockSpec(ov.shape, lambda i, pt: (0, 0)),
            pl.BlockSpec(e.shape, lambda i, pt: (0, 0)),
        ],
        out_specs=per_b(tq, Q_W),
        scratch_shapes=[
            pltpu.VMEM((2, KV_W, past), F32),
            pltpu.VMEM((2, KV_W, past), F32),
            pltpu.SemaphoreType.DMA((2,)),
            pltpu.SemaphoreType.DMA((2,)),
            pltpu.VMEM((KV_W + ns_past, past), BF16),
        ],
    )
    return pl.pallas_call(
        functools.partial(_nsa_sample_body, n_pages=n_pages, tq=tq),
        grid_spec=grid_spec,
        out_shape=jax.ShapeDtypeStruct((b, tq, Q_W), BF16),
        compiler_params=_cparams(("arbitrary",)),
        name="nsa_sample",
    )(page_table.reshape(-1), q.reshape(b, tq, Q_W), qr.reshape(b, tq, Q_W), ng.reshape(b, tq, LANES), kc, vc,
      kvb.reshape(b, tq, 6 * KV_W), k_win_t, v_win_t, pool_k_t, pool_v_t, ov, e)


def _merge_body(x_ref, g_ref, yc_ref, o_ref, wmg_ref, wn_ref, wo_ref, h_ref):
    x = x_ref[...]
    xn = _rms(x, g_ref[...]).astype(BF16)
    mg = jax.nn.sigmoid(_dot(xn, wmg_ref[...]))
    y_nsa = _dot(o_ref[...], wn_ref[...])
    mix = mg[:, :D_MODEL] * yc_ref[...] + mg[:, D_MODEL:] * y_nsa
    h_ref[...] = x + _dot(mix.astype(BF16), wo_ref[...])


def _merge(x2d, g_mix, y_conv, o, w_in, w_nsa_out, w_o, tm):
    n = x2d.shape[0]
    assert n % tm == 0
    wmg = w_in[:, _OFF_MG:_IN_COLS].astype(BF16)
    wn = w_nsa_out.astype(BF16)
    wo = w_o.astype(BF16)
    row = lambda w: pl.BlockSpec((tm, w), lambda i: (i, 0))
    full = lambda a: pl.BlockSpec(a.shape, lambda i: (0, 0))
    return pl.pallas_call(
        _merge_body,
        grid=(n // tm,),
        in_specs=[row(D_MODEL), full(g_mix), row(D_MODEL), row(Q_W), full(wmg), full(wn), full(wo)],
        out_specs=row(D_MODEL),
        out_shape=jax.ShapeDtypeStruct((n, D_MODEL), F32),
        compiler_params=_cparams(("parallel",)),
        name="merge",
    )(x2d, g_mix, y_conv, o, wmg, wn, wo)


_EXP_CHUNK = 2


def _moe_body(h_ref, gf_ref, wr_hi_ref, wr_mid_ref, wr_lo_ref, br_ref, wg_ref, wu_ref, wd_ref, gfin_ref, y_ref):
    h = h_ref[...]
    xn = _rms(h, gf_ref[...])
    tm = h.shape[0]

    x_hi, x_mid, x_lo = _split3(xn)
    w_hi, w_mid, w_lo = wr_hi_ref[...], wr_mid_ref[...], wr_lo_ref[...]
    logits = (_dot(x_hi, w_hi) + (_dot(x_hi, w_mid) + _dot(x_mid, w_hi))
              + (_dot(x_hi, w_lo) + _dot(x_mid, w_mid) + _dot(x_lo, w_hi))) + br_ref[...]
    lane = lax.broadcasted_iota(jnp.int32, logits.shape, 1)
    is_grp = lane < N_GROUPS
    lg = jnp.where(is_grp, logits, -jnp.inf)
    lg_max = jnp.max(lg, axis=-1, keepdims=True)
    g_star = jnp.min(jnp.where(lg == lg_max, lane, LANES), axis=-1, keepdims=True)
    p_grp = 1.0 / jnp.sum(jnp.where(is_grp, jnp.exp(lg - lg_max), 0.0), axis=-1, keepdims=True)
    eid = lane - N_GROUPS
    in_grp = (eid >= g_star * EXP_PER_GROUP) & (eid < (g_star + 1) * EXP_PER_GROUP)
    le = jnp.where(in_grp, logits, -jnp.inf)
    v1 = jnp.max(le, axis=-1, keepdims=True)
    i1 = jnp.min(jnp.where(le == v1, lane, LANES), axis=-1, keepdims=True)
    le2 = jnp.where(lane == i1, -jnp.inf, le)
    v2 = jnp.max(le2, axis=-1, keepdims=True)
    i2 = jnp.min(jnp.where(le2 == v2, lane, LANES), axis=-1, keepdims=True)
    e2 = jnp.exp(v2 - v1)
    w1 = p_grp / (1.0 + e2)
    w2 = p_grp * e2 / (1.0 + e2)
    comb = jnp.where(lane == i1, w1, 0.0) + jnp.where(lane == i2, w2, 0.0)

    xb = xn.astype(BF16)
    cw = _EXP_CHUNK * D_EXPERT
    y = jnp.zeros((tm, D_MODEL), F32)
    for c in range(N_EXPERTS // _EXP_CHUNK):
        hid = _dot(xb, wg_ref[:, c * cw:(c + 1) * cw])
        hid = hid * jax.nn.sigmoid(hid) * _dot(xb, wu_ref[:, c * cw:(c + 1) * cw])
        parts = []
        for j in range(_EXP_CHUNK):
            e_lane = N_GROUPS + c * _EXP_CHUNK + j
            parts.append(hid[:, j * D_EXPERT:(j + 1) * D_EXPERT] * comb[:, e_lane:e_lane + 1])
        hid = jnp.concatenate(parts, axis=1).astype(BF16)
        y = y + _dot(hid, wd_ref[c * cw:(c + 1) * cw, :])
    y_ref[...] = _rms(h + y, gfin_ref[...])


def _moe(h2d, g_ffn, w_rg, b_rg, w_re, b_re, w_gate, w_up, w_down, g_final, tm):
    n = h2d.shape[0]
    assert n % tm == 0
    w_r = jnp.pad(jnp.concatenate([w_rg, w_re], axis=1), ((0, 0), (0, LANES - N_GROUPS - N_EXPERTS)))
    wr_hi, wr_mid, wr_lo = _split3(w_r)
    b_r = jnp.pad(jnp.concatenate([b_rg, b_re]), (0, LANES - N_GROUPS - N_EXPERTS)).reshape(1, LANES)
    wg = jnp.transpose(w_gate, (1, 0, 2)).reshape(D_MODEL, N_EXPERTS * D_EXPERT).astype(BF16)
    wu = jnp.transpose(w_up, (1, 0, 2)).reshape(D_MODEL, N_EXPERTS * D_EXPERT).astype(BF16)
    wd = w_down.reshape(N_EXPERTS * D_EXPERT, D_MODEL).astype(BF16)
    row = pl.BlockSpec((tm, D_MODEL), lambda i: (i, 0))
    full = lambda a: pl.BlockSpec(a.shape, lambda i: (0, 0), pipeline_mode=pl.Buffered(1))
    return pl.pallas_call(
        _moe_body,
        grid=(n // tm,),
        in_specs=[row, full(g_ffn), full(wr_hi), full(wr_mid), full(wr_lo), full(b_r), full(wg), full(wu), full(wd),
                  full(g_final)],
        out_specs=row,
        out_shape=jax.ShapeDtypeStruct((n, D_MODEL), F32),
        compiler_params=_cparams(("parallel",)),
        name="moe",
    )(h2d, g_ffn, wr_hi, wr_mid, wr_lo, b_r, wg, wu, wd, g_final)


def _to_feature_major(x):
    b, t = x.shape[:2]
    return jnp.transpose(x, (0, 2, 3, 1)).reshape(b, KV_W, t)


def _from_feature_major(xt):
    b, _, t = xt.shape
    return jnp.transpose(xt.reshape(b, N_KV, HEAD_DIM, t), (0, 3, 1, 2))


def _layer_prompt(x, l, g_mix, w_in, conv_w, cmp_w, w_nsa_out, w_o, moe_w, g_fin):
    b, t, _ = x.shape
    x2d = x.reshape(b * t, D_MODEL)
    gm = g_mix.reshape(1, D_MODEL)
    (a, qt, qrt, kct, vct, kst, vst, kwt, vwt, ngt, kc_rm, vc_rm, ks_aug, kw_rm, vstb, vwtb) = _proj_in_t(x, gm, w_in, tm=512)
    y_conv = _conv(a.reshape(b, t, C_CONV), None, *conv_w, bb=1, tt=256)
    kc, vc_t = _compress_prompt(kc_rm, vc_rm, b, t, cmp_w)
    o = _nsa_prompt(qt, qrt, ngt, kc, vc_t, ks_aug, kw_rm, vstb, vwtb, b, t, qb=256, kc_len=512)
    h = _merge(x2d, gm, y_conv.reshape(b * t, D_MODEL), o.reshape(b * t, Q_W), w_in, w_nsa_out, w_o, tm=512)
    y = _moe(h, *moe_w, g_fin, tm=512)
    wb = min(WINDOW, t)
    state = (_from_feature_major(kct), _from_feature_major(vct), _from_feature_major(kst), _from_feature_major(vst),
             _from_feature_major(kwt[:, :, t - wb:]), _from_feature_major(vwt[:, :, t - wb:]),
             a.reshape(b, t, C_CONV)[:, t - (CONV_K - 1):])
    return y.reshape(b, t, D_MODEL), state


def _layer_sample(x, ck_cmp, cv_cmp, ck_sel, cv_sel, sk_win, sv_win, s_conv, page_table,
                  g_mix, w_in, conv_w, cmp_w, w_nsa_out, w_o, moe_w, g_fin):
    b, t, _ = x.shape
    past = page_table.shape[1] * PAGE_SIZE
    x2d = x.reshape(b * t, D_MODEL)
    gm = g_mix.reshape(1, D_MODEL)
    a, q, qr, kc_r, vc_r, ks, vs, kw, vw, ng, kvb = _proj_in(x2d, gm, w_in, past + jnp.arange(t), tm=256)
    a3 = a.reshape(b, t, C_CONV)
    hist = jnp.pad(s_conv, ((0, 0), (_HALO - (CONV_K - 1), 0), (0, 0)))
    y_conv = _conv(a3, hist, *conv_w, bb=16, tt=t)
    kc, vc = _compress_sample(_to_feature_major(ck_cmp), _to_feature_major(cv_cmp), page_table, cmp_w)
    sk_t, sv_t = _to_feature_major(sk_win), _to_feature_major(sv_win)
    o = _nsa_sample(q, qr, ng, kc, vc, kvb, sk_t, sv_t, _to_feature_major(ck_sel), _to_feature_major(cv_sel),
                    page_table, tq=t)
    h = _merge(x2d, gm, y_conv.reshape(b * t, D_MODEL), o.reshape(b * t, Q_W), w_in, w_nsa_out, w_o, tm=256)
    y = _moe(h, *moe_w, g_fin, tm=256)
    heads = lambda z: z.reshape(b, t, N_KV, HEAD_DIM)
    k_win = _from_feature_major(jnp.concatenate([sk_t, _to_feature_major(heads(kw))], axis=2)[:, :, t:])
    v_win = _from_feature_major(jnp.concatenate([sv_t, _to_feature_major(heads(vw))], axis=2)[:, :, t:])
    conv_state = jnp.concatenate([s_conv, a3], axis=1)[:, -(CONV_K - 1):]
    state = (heads(kc_r), heads(vc_r), heads(ks), heads(vs), k_win, v_win, conv_state)
    return y.reshape(b, t, D_MODEL), state


def kernel(x_prompt, x_sample, cache_k_cmp, cache_v_cmp, cache_k_sel, cache_v_sel, state_k_win, state_v_win,
           state_conv, page_table, g_mix, w_in, w_dw, b_dw, conv_norm_g, conv_norm_b, w_conv_out,
           w_ck1, b_ck1, w_ck2, w_cv1, b_cv1, w_cv2, w_nsa_out, w_o, g_ffn, w_rg, b_rg, w_re, b_re,
           w_gate, w_up, w_down, g_final):
    depth = g_mix.shape[0]
    assert depth == 1, "single-layer trunk"
    l = 0
    conv_w = (w_dw[l], b_dw[l], conv_norm_g[l], conv_norm_b[l], w_conv_out[l])
    cmp_w = (w_ck1[l], b_ck1[l], w_ck2[l], w_cv1[l], b_cv1[l], w_cv2[l])
    moe_w = (g_ffn[l].reshape(1, D_MODEL), w_rg[l], b_rg[l], w_re[l], b_re[l], w_gate[l], w_up[l], w_down[l])
    g_fin = g_final.reshape(1, D_MODEL)
    y_p, st_p = _layer_prompt(x_prompt, l, g_mix[l], w_in[l], conv_w, cmp_w, w_nsa_out[l], w_o[l], moe_w, g_fin)
    y_s, st_s = _layer_sample(x_sample, cache_k_cmp[l], cache_v_cmp[l], cache_k_sel[l], cache_v_sel[l],
                              state_k_win[l], state_v_win[l], state_conv[l], page_table,
                              g_mix[l], w_in[l], conv_w, cmp_w, w_nsa_out[l], w_o[l], moe_w, g_fin)
    stack = lambda z: z[None]
    return (y_p, y_s) + tuple(stack(z) for z in st_p) + tuple(stack(z) for z in st_s)
```
